```python
import jax, jax.numpy as jnp
from jax import lax
import numpy as np

D_MODEL = 1024
BATCH = 16
SEQ = 2048
DEPTH = 2

N_MIXERS = 2
RMS_EPS = 1e-6

SB_HEADS = 8
SB_HEAD_DIM = D_MODEL // SB_HEADS
SB_WIDTH = SB_HEADS * SB_HEAD_DIM
Q_BLOCK = 128

HG_HEAD_K = 128
HG_HEADS = D_MODEL // HG_HEAD_K
HG_HEAD_V = D_MODEL // HG_HEADS
HG_KEY_WIDTH = HG_HEADS * HG_HEAD_K
HG_VALUE_WIDTH = HG_HEADS * HG_HEAD_V
CHUNK = 64

kernel_name = "hybrid_stickbreaking_hgrn2_trunk"


def rms_norm(x, gain):
    xf = x.astype(jnp.float32)
    y = xf * lax.rsqrt(jnp.mean(xf * xf, axis=-1, keepdims=True) + RMS_EPS)
    return (y * gain.astype(jnp.float32)).astype(x.dtype)


def split_heads(t, n_heads):
    b, s, _ = t.shape
    return t.reshape(b, s, n_heads, -1).transpose(0, 2, 1, 3)


def merge_heads(t):
    b, h, s, d = t.shape
    return t.transpose(0, 2, 1, 3).reshape(b, s, h * d)


def stick_breaking_attention(q, k, v):
    seq = q.shape[2]
    scale = q.shape[-1] ** -0.5
    outs = []
    for blk in range(seq // Q_BLOCK):
        start = blk * Q_BLOCK
        end = start + Q_BLOCK
        qb = q[:, :, start:end]
        kb = k[:, :, :end]
        vb = v[:, :, :end]
        z = jnp.einsum('bhtd,bhsd->bhts', qb, kb).astype(jnp.float32) * scale
        t_idx = start + jnp.arange(Q_BLOCK)[:, None]
        s_idx = jnp.arange(end)[None, :]
        causal = s_idx < t_idx
        log_beta = jax.nn.log_sigmoid(z)
        log_skip_j = jnp.where(causal, jax.nn.log_sigmoid(-z), 0.0)
        log_skip = lax.cumsum(log_skip_j, axis=3, reverse=True) - log_skip_j
        weights = jnp.where(causal, jnp.exp(log_beta + log_skip), 0.0)
        outs.append(jnp.einsum('bhts,bhsd->bhtd', weights.astype(vb.dtype), vb))
    return jnp.concatenate(outs, axis=2)


def hgrn2_chunkwise(q, log_f, k, i):
    b, h, s, dk = q.shape
    dv = i.shape[-1]
    n = s // CHUNK

    def to_chunks(a):
        return a.astype(jnp.float32).reshape(b, h, n, CHUNK, a.shape[-1]).transpose(2, 0, 1, 3, 4)

    qc, gc, kc, ic = to_chunks(q), to_chunks(log_f), to_chunks(k), to_chunks(i)
    incl = jnp.tril(jnp.ones((CHUNK, CHUNK), dtype=bool))[None, None, :, :, None]

    def step(state, xs):
        qb, gb, kb, ib = xs
        g_cum = jnp.cumsum(gb, axis=2)
        rel = g_cum[:, :, :, None, :] - g_cum[:, :, None, :, :]
        decay = jnp.exp(jnp.where(incl, rel, -jnp.inf))
        scores = jnp.einsum('bhtd,bhsd,bhtsd->bhts', qb, kb, decay)
        o_intra = jnp.einsum('bhts,bhsv->bhtv', scores, ib)
        o_inter = jnp.einsum('bhtd,bhdv->bhtv', qb * jnp.exp(g_cum), state)
        g_last = g_cum[:, :, -1:, :]
        new_state = (jnp.exp(g_last[:, :, 0, :])[..., None] * state
                     + jnp.einsum('bhsd,bhsv->bhdv', kb * jnp.exp(g_last - g_cum), ib))
        return new_state, o_intra + o_inter

    state0 = jnp.zeros((b, h, dk, dv), jnp.float32)
    _, oc = lax.scan(step, state0, (qc, gc, kc, ic))
    return oc.transpose(1, 2, 0, 3, 4).reshape(b, h, s, dv).astype(i.dtype)


def stick_breaking_layer(h, norm_gain, w_in, q_gain, k_gain, w_out):
    u = rms_norm(h, norm_gain)
    proj = u @ w_in
    q, k, v, gate = jnp.split(proj, 4, axis=-1)
    q = rms_norm(split_heads(q, SB_HEADS), q_gain)
    k = rms_norm(split_heads(k, SB_HEADS), k_gain)
    v = split_heads(v, SB_HEADS)
    o = merge_heads(stick_breaking_attention(q, k, v))
    return (o * jax.nn.silu(gate)) @ w_out


def hgrn2_layer(h, norm_gain, w_in, o_gain, w_out, lower_bound):
    u = rms_norm(h, norm_gain)
    proj = u @ w_in
    q, f_pre, i, gate = jnp.split(
        proj, [HG_KEY_WIDTH, 2 * HG_KEY_WIDTH, 2 * HG_KEY_WIDTH + HG_VALUE_WIDTH], axis=-1)
    q = split_heads(jax.nn.silu(q), HG_HEADS)
    f_pre = split_heads(f_pre, HG_HEADS).astype(jnp.float32)
    lb = lower_bound.astype(jnp.float32).reshape(HG_HEADS, 1, HG_HEAD_K)
    log_f = jnp.logaddexp(jnp.log(lb), jnp.log1p(-lb) + jax.nn.log_sigmoid(f_pre))
    k_in = -jnp.expm1(log_f)
    o = hgrn2_chunkwise(q, log_f, k_in, split_heads(i, HG_HEADS))
    o = merge_heads(rms_norm(o, o_gain))
    return (o * jax.nn.silu(gate)) @ w_out


def setup_inputs(seed: int = 0) -> dict:
    key = jax.random.key(seed)
    ks = jax.random.split(key, 12)
    n_a = (DEPTH + 1) // 2
    n_b = DEPTH // 2

    def normal(k, shape, scale):
        return jax.random.normal(k, shape, jnp.float32) * scale

    return {
        "x": normal(ks[0], (BATCH, SEQ, D_MODEL), 1.0),
        "sb_norm": 1.0 + normal(ks[1], (n_a, D_MODEL), 0.02),
        "sb_w_in": normal(ks[2], (n_a, D_MODEL, 4 * SB_WIDTH), D_MODEL ** -0.5),
        "sb_q_gain": 1.0 + normal(ks[3], (n_a, SB_HEAD_DIM), 0.02),
        "sb_k_gain": 1.0 + normal(ks[4], (n_a, SB_HEAD_DIM), 0.02),
        "sb_w_out": normal(ks[5], (n_a, SB_WIDTH, D_MODEL), SB_WIDTH ** -0.5),
        "hg_norm": 1.0 + normal(ks[6], (n_b, D_MODEL), 0.02),
        "hg_w_in": normal(ks[7], (n_b, D_MODEL, 2 * HG_KEY_WIDTH + 2 * HG_VALUE_WIDTH), D_MODEL ** -0.5),
        "hg_o_gain": 1.0 + normal(ks[8], (n_b, HG_HEAD_V), 0.02),
        "hg_w_out": normal(ks[9], (n_b, HG_VALUE_WIDTH, D_MODEL), HG_VALUE_WIDTH ** -0.5),
        "hg_lb_logits": normal(ks[10], (DEPTH, HG_KEY_WIDTH), 0.1),
    }


def reference(x, sb_norm, sb_w_in, sb_q_gain, sb_k_gain, sb_w_out,
              hg_norm, hg_w_in, hg_o_gain, hg_w_out, hg_lb_logits):
    lb_p = jax.nn.softmax(hg_lb_logits.astype(jnp.float32), axis=0)
    lb_table = jnp.cumsum(lb_p, axis=0) - lb_p[0]
    h = x
    for layer in range(DEPTH):
        j = layer // N_MIXERS
        if layer % N_MIXERS == 0:
            h = h + stick_breaking_layer(h, sb_norm[j], sb_w_in[j], sb_q_gain[j],
                                         sb_k_gain[j], sb_w_out[j])
        else:
            h = h + hgrn2_layer(h, hg_norm[j], hg_w_in[j], hg_o_gain[j], hg_w_out[j],
                                lb_table[layer])
    return h
```

```python
import functools

import jax
import jax.numpy as jnp
from jax import lax
from jax.experimental import pallas as pl
from jax.experimental.pallas import tpu as pltpu

HEAD_DIM = 128
RMS_EPS = 1e-6
ROW_TILE = 256
HG_CHUNK = 128
HG_SUBCHUNK = 8
HG_MAX_CHUNK_DECAY = 80.0
V7X_VMEM_LIMIT_BYTES = 56 * 1024 * 1024

_F32 = jnp.float32
_BF16 = jnp.bfloat16
_NT = (((1,), (1,)), ((), ()))
_TN = (((0,), (0,)), ((), ()))


def _rms_rows(x, gain):
    return x * lax.rsqrt(jnp.mean(x * x, axis=-1, keepdims=True) + RMS_EPS) * gain


def _silu(x):
    return x / (1.0 + jnp.exp(-x))


def _split_dot(x, ones_tri, dims=None):
    hi = x.astype(_BF16)
    lo = (x - hi.astype(_F32)).astype(_BF16)
    if dims is None:
        return (jnp.dot(hi, ones_tri, preferred_element_type=_F32)
                + jnp.dot(lo, ones_tri, preferred_element_type=_F32))
    return (lax.dot_general(ones_tri, hi, dims, preferred_element_type=_F32)
            + lax.dot_general(ones_tri, lo, dims, preferred_element_type=_F32))


def _project(x, gain, win_ref, width, store):
    u = _rms_rows(x, gain).astype(_BF16)
    n_heads = width // HEAD_DIM
    for grp in range(win_ref.shape[1] // width):
        pg = jnp.dot(u, win_ref[:, grp * width:(grp + 1) * width],
                     preferred_element_type=_F32)
        for h in range(n_heads):
            store(grp, h, pg[:, h * HEAD_DIM:(h + 1) * HEAD_DIM])


def _gated_out(x, o_scr, gate_scr, wout_ref, n_heads):
    o = jnp.concatenate([o_scr[h] * gate_scr[h] for h in range(n_heads)], axis=1)
    return x + jnp.dot(o.astype(_BF16), wout_ref[...], preferred_element_type=_F32)


def _sb_layer_kernel(x_ref, ng_ref, win_ref, qg_ref, kg_ref, wout_ref, out_ref,
                     q_scr, k_scr, v_scr, gate_scr, o_scr, *, n_heads):
    tq = x_ref.shape[1]
    i = pl.program_id(1)
    row0 = pl.multiple_of(i * tq, tq)
    x = x_ref[0]
    qg = qg_ref[...]
    kg = kg_ref[...]

    def store(grp, h, tile):
        if grp == 0:
            q_scr[h] = _rms_rows(tile, qg).astype(_BF16)
        elif grp == 1:
            k_scr[h, pl.ds(row0, tq), :] = _rms_rows(tile, kg).astype(_BF16)
        elif grp == 2:
            v_scr[h, pl.ds(row0, tq), :] = tile.astype(_BF16)
        else:
            gate_scr[h] = _silu(tile)

    _project(x, ng_ref[...], win_ref, n_heads * HEAD_DIM, store)

    scale = HEAD_DIM ** -0.5
    t_idx = lax.broadcasted_iota(jnp.int32, (tq, tq), 0)
    s_idx = lax.broadcasted_iota(jnp.int32, (tq, tq), 1)
    causal = s_idx < t_idx
    later_keys = jnp.where(s_idx < t_idx, 1.0, 0.0).astype(_BF16)

    def log_gates(q, k):
        z = lax.dot_general(q, k, _NT, preferred_element_type=_F32) * scale
        softplus_tail = jnp.log(1.0 + jnp.exp(-jnp.abs(z)))
        log_beta = jnp.minimum(z, 0.0) - softplus_tail
        log_skip = -jnp.maximum(z, 0.0) - softplus_tail
        return log_beta, log_skip

    def head_body(h, carry_unused):
        q = q_scr[h]
        log_beta, log_skip = log_gates(q, k_scr[h, pl.ds(row0, tq), :])
        log_skip = jnp.where(causal, log_skip, 0.0)
        skipped = _split_dot(log_skip, later_keys)
        w = jnp.where(causal, jnp.exp(log_beta + skipped), 0.0)
        acc = jnp.dot(w.astype(_BF16), v_scr[h, pl.ds(row0, tq), :],
                      preferred_element_type=_F32)
        skip_sum = jnp.sum(log_skip, axis=-1, keepdims=True)

        def kv_body(n, state):
            acc, skip_sum = state
            r0 = pl.multiple_of((i - 1 - n) * tq, tq)
            log_beta, log_skip = log_gates(q, k_scr[h, pl.ds(r0, tq), :])
            skipped = _split_dot(log_skip, later_keys) + skip_sum
            w = jnp.exp(log_beta + skipped)
            acc = acc + jnp.dot(w.astype(_BF16), v_scr[h, pl.ds(r0, tq), :],
                                preferred_element_type=_F32)
            skip_sum = skip_sum + jnp.sum(log_skip, axis=-1, keepdims=True)
            return acc, skip_sum

        acc, _ = lax.fori_loop(0, i, kv_body, (acc, skip_sum))
        o_scr[h] = acc
        return carry_unused

    lax.fori_loop(0, n_heads, head_body, 0)
    out_ref[0] = _gated_out(x, o_scr, gate_scr, wout_ref, n_heads)


def _hg_layer_kernel(x_ref, ng_ref, win_ref, og_ref, wout_ref, lbl_ref, out_ref,
                     q_scr, g_scr, k_scr, i_scr, gate_scr, o_scr, st_scr,
                     *, n_heads, layer):
    tm = x_ref.shape[1]
    n_chunks = tm // HG_CHUNK
    x = x_ref[0]
    og = og_ref[...]

    @pl.when(pl.program_id(1) == 0)
    def _():
        st_scr[...] = jnp.zeros_like(st_scr)

    lg = lbl_ref[...]
    lmax = jnp.max(lg, axis=0, keepdims=True)
    le = jnp.exp(lg - lmax)
    lb_row = jnp.sum(le[1:layer + 1], axis=0, keepdims=True) / jnp.sum(le, axis=0, keepdims=True)

    chunk_min = [None]

    def store(grp, h, tile):
        if grp == 0:
            q_scr[h] = _silu(tile)
        elif grp == 1:
            lb = lb_row[:, h * HEAD_DIM:(h + 1) * HEAD_DIM]
            e = jnp.exp(-jnp.abs(tile))
            r = 1.0 / (1.0 + e)
            sig = jnp.where(tile >= 0.0, r, e * r)
            nsig = jnp.where(tile >= 0.0, e * r, r)
            g = jnp.log(lb + (1.0 - lb) * sig)
            g_scr[h] = g
            k_scr[h] = (1.0 - lb) * nsig
            for c in range(n_chunks):
                tot = jnp.sum(g[c * HG_CHUNK:(c + 1) * HG_CHUNK], axis=0, keepdims=True)
                chunk_min[0] = tot if chunk_min[0] is None else jnp.minimum(chunk_min[0], tot)
        elif grp == 2:
            i_scr[h] = tile
        else:
            gate_scr[h] = _silu(tile)

    _project(x, ng_ref[...], win_ref, n_heads * HEAD_DIM, store)
    matmul_path_ok = jnp.min(chunk_min[0]) >= -HG_MAX_CHUNK_DECAY

    def finish_head(h, o):
        o_scr[h] = _rms_rows(o, og)

    def matmul_path():
        c_t = lax.broadcasted_iota(jnp.int32, (HG_CHUNK, HG_CHUNK), 0)
        c_s = lax.broadcasted_iota(jnp.int32, (HG_CHUNK, HG_CHUNK), 1)
        incl = c_s <= c_t
        earlier_rows = jnp.where(incl, 1.0, 0.0).astype(_BF16)

        def head_body(h, carry_unused):
            st = st_scr[h]
            outs = []
            for c in range(n_chunks):
                rows = pl.ds(c * HG_CHUNK, HG_CHUNK)
                q = q_scr[h, rows, :]
                k = k_scr[h, rows, :]
                iv = i_scr[h, rows, :].astype(_BF16)
                g = g_scr[h, rows, :]
                hi = g.astype(_BF16)
                lo = (g - hi.astype(_F32)).astype(_BF16)
                gc = (jnp.dot(earlier_rows, hi, preferred_element_type=_F32)
                      + jnp.dot(earlier_rows, lo, preferred_element_type=_F32))
                g_last = gc[HG_CHUNK - 1:HG_CHUNK, :]
                g_mid = gc[HG_CHUNK // 2 - 1:HG_CHUNK // 2, :]
                q_mid = (q * jnp.exp(gc - g_mid)).astype(_BF16)
                k_mid = (k * jnp.exp(g_mid - gc)).astype(_BF16)
                scores = lax.dot_general(q_mid, k_mid, _NT, preferred_element_type=_F32)
                scores = jnp.where(incl, scores, 0.0).astype(_BF16)
                o = jnp.dot(scores, iv, preferred_element_type=_F32)
                o = o + lax.dot_general((q * jnp.exp(gc)).astype(_BF16), st.astype(_BF16),
                                        _NT, preferred_element_type=_F32)
                k_end = (k * jnp.exp(g_last - gc)).astype(_BF16)
                st = jnp.exp(g_last) * st + lax.dot_general(
                    iv, k_end, _TN, preferred_element_type=_F32)
                outs.append(o)
            st_scr[h] = st
            finish_head(h, jnp.concatenate(outs, axis=0) if n_chunks > 1 else outs[0])
            return carry_unused

        lax.fori_loop(0, n_heads, head_body, 0)

    def pairwise_path():
        sc = HG_SUBCHUNK
        r_t = lax.broadcasted_iota(jnp.int32, (sc, HEAD_DIM), 0)

        def head_body(h, carry_unused):
            def sub_body(c, st):
                rows = pl.ds(pl.multiple_of(c * sc, sc), sc)
                q = q_scr[h, rows, :]
                k = k_scr[h, rows, :]
                iv = i_scr[h, rows, :]
                g = g_scr[h, rows, :]
                gc = g
                for shift in (1, 2, 4):
                    moved = pltpu.roll(gc, shift, 0)
                    gc = gc + jnp.where(r_t >= shift, moved, 0.0)
                g_last = gc[sc - 1:sc, :]
                o = lax.dot_general((q * jnp.exp(gc)).astype(_BF16), st.astype(_BF16),
                                    _NT, preferred_element_type=_F32)
                for s in range(sc):
                    decay = jnp.exp(jnp.minimum(gc - gc[s:s + 1, :], 0.0))
                    pair = jnp.sum(q * k[s:s + 1, :] * decay, axis=-1, keepdims=True)
                    o = o + jnp.where(r_t >= s, pair, 0.0) * iv[s:s + 1, :]
                k_end = (k * jnp.exp(g_last - gc)).astype(_BF16)
                st = jnp.exp(g_last) * st + lax.dot_general(
                    iv.astype(_BF16), k_end, _TN, preferred_element_type=_F32)
                o_scr[h, rows, :] = o
                return st

            st_scr[h] = lax.fori_loop(0, tm // sc, sub_body, st_scr[h])
            finish_head(h, o_scr[h])
            return carry_unused

        lax.fori_loop(0, n_heads, head_body, 0)

    lax.cond(matmul_path_ok, matmul_path, pairwise_path)
    out_ref[0] = _gated_out(x, o_scr, gate_scr, wout_ref, n_heads)


def _layer_call(body, x, operands, scratch_shapes, name):
    batch, seq, d_model = x.shape
    tm = min(ROW_TILE, seq)
    assert seq % tm == 0 and tm % HG_CHUNK == 0

    def whole(a):
        return pl.BlockSpec(a.shape, lambda b, t: (0,) * a.ndim)

    tile = pl.BlockSpec((1, tm, d_model), lambda b, t: (b, t, 0))
    return pl.pallas_call(
        body,
        grid=(batch, seq // tm),
        in_specs=[tile] + [whole(a) for a in operands],
        out_specs=tile,
        out_shape=jax.ShapeDtypeStruct(x.shape, x.dtype),
        scratch_shapes=scratch_shapes(tm, seq),
        compiler_params=pltpu.CompilerParams(
            dimension_semantics=("arbitrary", "arbitrary"),
            vmem_limit_bytes=V7X_VMEM_LIMIT_BYTES),
        name=name,
    )(x, *operands)


def _sb_layer(x, norm_gain, w_in, q_gain, k_gain, w_out):
    width = w_out.shape[0]
    n_heads = width // HEAD_DIM
    assert w_in.shape[1] == 4 * width

    def scratch(tm, seq):
        return [
            pltpu.VMEM((n_heads, tm, HEAD_DIM), _BF16),
            pltpu.VMEM((n_heads, seq, HEAD_DIM), _BF16),
            pltpu.VMEM((n_heads, seq, HEAD_DIM), _BF16),
            pltpu.VMEM((n_heads, tm, HEAD_DIM), _F32),
            pltpu.VMEM((n_heads, tm, HEAD_DIM), _F32),
        ]

    operands = (norm_gain.reshape(1, -1), w_in.astype(_BF16), q_gain.reshape(1, -1),
                k_gain.reshape(1, -1), w_out.astype(_BF16))
    return _layer_call(functools.partial(_sb_layer_kernel, n_heads=n_heads),
                       x, operands, scratch, "stick_breaking_layer")


def _hg_layer(x, norm_gain, w_in, o_gain, w_out, lb_logits, layer):
    width = w_out.shape[0]
    n_heads = width // HEAD_DIM
    assert w_in.shape[1] == 4 * width

    def scratch(tm, seq):
        per_head = lambda dt: pltpu.VMEM((n_heads, tm, HEAD_DIM), dt)
        return [per_head(_F32)] * 6 + [pltpu.VMEM((n_heads, HEAD_DIM, HEAD_DIM), _F32)]

    operands = (norm_gain.reshape(1, -1), w_in.astype(_BF16), o_gain.reshape(1, -1),
                w_out.astype(_BF16), lb_logits)
    return _layer_call(functools.partial(_hg_layer_kernel, n_heads=n_heads, layer=layer),
                       x, operands, scratch, "hgrn2_layer")


def kernel(x, sb_norm, sb_w_in, sb_q_gain, sb_k_gain, sb_w_out,
           hg_norm, hg_w_in, hg_o_gain, hg_w_out, hg_lb_logits):
    depth = hg_lb_logits.shape[0]
    h = x
    for layer in range(depth):
        j = layer // 2
        if layer % 2 == 0:
            h = _sb_layer(h, sb_norm[j], sb_w_in[j], sb_q_gain[j], sb_k_gain[j], sb_w_out[j])
        else:
            h = _hg_layer(h, hg_norm[j], hg_w_in[j], hg_o_gain[j], hg_w_out[j],
                          hg_lb_logits, layer)
    return h
```

```python
import functools

import jax
import jax.numpy as jnp
from jax import lax
from jax.experimental import pallas as pl
from jax.experimental.pallas import tpu as pltpu

HEAD_DIM = 128
RMS_EPS = 1e-6
LOG2_E = 1.4426950408889634
ROW_TILE = 256
HG_CHUNK = 128
HG_SUBCHUNK = 8
HG_MAX_CHUNK_DECAY = 80.0
PIPELINE_SKEW = 2
V7X_VMEM_LIMIT_BYTES = 56 * 1024 * 1024

_F32 = jnp.float32
_BF16 = jnp.bfloat16
_NT = (((1,), (1,)), ((), ()))
_TN = (((0,), (0,)), ((), ()))


def _rms_rows(x, gain):
    return x * lax.rsqrt(jnp.mean(x * x, axis=-1, keepdims=True) + RMS_EPS) * gain


def _silu(x):
    return x / (1.0 + jnp.exp(-x))


def _project(x, gain, win_ref, width, store):
    u = _rms_rows(x, gain).astype(_BF16)
    n_heads = width // HEAD_DIM
    for grp in range(win_ref.shape[1] // width):
        pg = jnp.dot(u, win_ref[:, grp * width:(grp + 1) * width],
                     preferred_element_type=_F32)
        for h in range(n_heads):
            store(grp, h, pg[:, h * HEAD_DIM:(h + 1) * HEAD_DIM])


def _gated_out(x, o_scr, gate_scr, wout_ref, n_heads):
    o = jnp.concatenate([o_scr[h] * gate_scr[h] for h in range(n_heads)], axis=1)
    return x + jnp.dot(o.astype(_BF16), wout_ref[...], preferred_element_type=_F32)


def _sb_layer_kernel(x_ref, ng_ref, win_ref, qg_ref, kg_ref, wout_ref, out_ref,
                     q_scr, k_scr, v_scr, gate_scr, acc_scr, sum_scr, *, n_heads):
    tq = x_ref.shape[1]
    i = pl.program_id(1)
    row0 = pl.multiple_of(i * tq, tq)
    x = x_ref[0]
    qg = qg_ref[...]
    kg = kg_ref[...]

    def store(grp, h, tile):
        if grp == 0:
            q_scr[h] = (_rms_rows(tile, qg) * (HEAD_DIM ** -0.5 * LOG2_E)).astype(_BF16)
        elif grp == 1:
            k_scr[h, pl.ds(row0, tq), :] = _rms_rows(tile, kg).astype(_BF16)
        elif grp == 2:
            v_scr[h, pl.ds(row0, tq), :] = tile.astype(_BF16)
        else:
            gate_scr[h] = _silu(tile)

    _project(x, ng_ref[...], win_ref, n_heads * HEAD_DIM, store)

    t_idx = lax.broadcasted_iota(jnp.int32, (tq, tq), 0)
    s_idx = lax.broadcasted_iota(jnp.int32, (tq, tq), 1)
    causal = s_idx < t_idx
    from_key = jnp.where(s_idx <= t_idx, 1.0, 0.0).astype(_BF16)


    def gates(h, r0, diag):
        k = k_scr[h, pl.ds(r0, tq), :]
        z2 = lax.dot_general(q_scr[h], k, _NT, preferred_element_type=_F32)
        skip = jnp.maximum(z2, 0.0) + jnp.log(1.0 + jnp.exp2(-jnp.abs(z2))) * LOG2_E
        if diag:
            skip = jnp.where(causal, skip, 0.0)
        row_sum = jnp.sum(skip, axis=-1, keepdims=True)
        return z2, skip.astype(_BF16), row_sum

    def weights(h, z2, skip, row_sum, diag):
        skipped = jnp.dot(skip, from_key, preferred_element_type=_F32)
        if diag:
            sum_scr[h] = jnp.broadcast_to(row_sum, (tq, HEAD_DIM))
            return jnp.where(causal, jnp.exp2(z2 - skipped), 0.0).astype(_BF16)
        nearer = sum_scr[h]
        sum_scr[h] = nearer + row_sum
        return jnp.exp2(z2 - skipped - jnp.concatenate([nearer, nearer], axis=1)).astype(_BF16)

    def accumulate(h, r0, w, diag):
        o = jnp.dot(w, v_scr[h, pl.ds(r0, tq), :], preferred_element_type=_F32)
        if diag:
            acc_scr[h] = o
        else:
            acc_scr[h] += o

    def attend_all_heads(r0, diag):
        gated, weighted = {}, {}
        for step in range(n_heads + 2 * PIPELINE_SKEW):
            h1, h2, h3 = step, step - PIPELINE_SKEW, step - 2 * PIPELINE_SKEW
            if h1 < n_heads:
                gated[h1] = gates(h1, r0, diag)
            if 0 <= h2 < n_heads:
                weighted[h2] = weights(h2, *gated.pop(h2), diag)
            if 0 <= h3 < n_heads:
                accumulate(h3, r0, weighted.pop(h3), diag)

    attend_all_heads(row0, True)

    def kv_body(n, carry_unused):
        attend_all_heads(pl.multiple_of((i - 1 - n) * tq, tq), False)
        return carry_unused

    lax.fori_loop(0, i, kv_body, 0)
    out_ref[0] = _gated_out(x, acc_scr, gate_scr, wout_ref, n_heads)


def _hg_layer_kernel(x_ref, ng_ref, win_ref, og_ref, wout_ref, lbl_ref, out_ref,
                     q_scr, g_scr, k_scr, i_scr, gate_scr, o_scr, st_scr,
                     *, n_heads, layer):
    tm = x_ref.shape[1]
    n_chunks = tm // HG_CHUNK
    x = x_ref[0]
    og = og_ref[...]

    @pl.when(pl.program_id(1) == 0)
    def _():
        st_scr[...] = jnp.zeros_like(st_scr)

    lg = lbl_ref[...]
    lmax = jnp.max(lg, axis=0, keepdims=True)
    le = jnp.exp(lg - lmax)
    lb_row = jnp.sum(le[1:layer + 1], axis=0, keepdims=True) / jnp.sum(le, axis=0, keepdims=True)

    chunk_min = [None]

    def store(grp, h, tile):
        if grp == 0:
            q_scr[h] = _silu(tile)
        elif grp == 1:
            lb = lb_row[:, h * HEAD_DIM:(h + 1) * HEAD_DIM]
            e = jnp.exp(-jnp.abs(tile))
            r = 1.0 / (1.0 + e)
            sig = jnp.where(tile >= 0.0, r, e * r)
            nsig = jnp.where(tile >= 0.0, e * r, r)
            g = jnp.log(lb + (1.0 - lb) * sig)
            g_scr[h] = g
            k_scr[h] = (1.0 - lb) * nsig
            for c in range(n_chunks):
                tot = jnp.sum(g[c * HG_CHUNK:(c + 1) * HG_CHUNK], axis=0, keepdims=True)
                chunk_min[0] = tot if chunk_min[0] is None else jnp.minimum(chunk_min[0], tot)
        elif grp == 2:
            i_scr[h] = tile
        else:
            gate_scr[h] = _silu(tile)

    _project(x, ng_ref[...], win_ref, n_heads * HEAD_DIM, store)
    matmul_path_ok = jnp.min(chunk_min[0]) >= -HG_MAX_CHUNK_DECAY

    def finish_head(h, o):
        o_scr[h] = _rms_rows(o, og)

    def matmul_path():
        c_t = lax.broadcasted_iota(jnp.int32, (HG_CHUNK, HG_CHUNK), 0)
        c_s = lax.broadcasted_iota(jnp.int32, (HG_CHUNK, HG_CHUNK), 1)
        incl = c_s <= c_t
        earlier_rows = jnp.where(incl, 1.0, 0.0).astype(_BF16)


        def decays(c, h):
            rows = pl.ds(c * HG_CHUNK, HG_CHUNK)
            q = q_scr[h, rows, :]
            k = k_scr[h, rows, :]
            g = g_scr[h, rows, :]
            hi = g.astype(_BF16)
            lo = (g - hi.astype(_F32)).astype(_BF16)
            gc = (jnp.dot(earlier_rows, hi, preferred_element_type=_F32)
                  + jnp.dot(earlier_rows, lo, preferred_element_type=_F32))
            g_last = gc[HG_CHUNK - 1:HG_CHUNK, :]
            g_mid = gc[HG_CHUNK // 2 - 1:HG_CHUNK // 2, :]
            q_mid = (q * jnp.exp(gc - g_mid)).astype(_BF16)
            k_mid = (k * jnp.exp(g_mid - gc)).astype(_BF16)
            q_in = (q * jnp.exp(gc)).astype(_BF16)
            k_end = (k * jnp.exp(g_last - gc)).astype(_BF16)
            return q_mid, k_mid, q_in, k_end, jnp.exp(g_last)

        def intra_scores(q_mid, k_mid):
            scores = lax.dot_general(q_mid, k_mid, _NT, preferred_element_type=_F32)
            return jnp.where(incl, scores, 0.0).astype(_BF16)

        def outputs(c, h, scores, q_in, k_end, decay_last):
            rows = pl.ds(c * HG_CHUNK, HG_CHUNK)
            iv = i_scr[h, rows, :].astype(_BF16)
            st = st_scr[h]
            o = jnp.dot(scores, iv, preferred_element_type=_F32)
            o = o + lax.dot_general(q_in, st.astype(_BF16), _NT, preferred_element_type=_F32)
            st_scr[h] = decay_last * st + lax.dot_general(
                iv, k_end, _TN, preferred_element_type=_F32)
            o_scr[h, rows, :] = _rms_rows(o, og)

        items = [(c, h) for c in range(n_chunks) for h in range(n_heads)]
        stage1, stage2 = {}, {}
        for step in range(len(items) + 2 * PIPELINE_SKEW):
            n1, n2, n3 = step, step - PIPELINE_SKEW, step - 2 * PIPELINE_SKEW
            if n1 < len(items):
                stage1[n1] = decays(*items[n1])
            if 0 <= n2 < len(items):
                q_mid, k_mid, q_in, k_end, decay_last = stage1.pop(n2)
                stage2[n2] = (intra_scores(q_mid, k_mid), q_in, k_end, decay_last)
            if 0 <= n3 < len(items):
                outputs(*items[n3], *stage2.pop(n3))

    def pairwise_path():
        sc = HG_SUBCHUNK
        r_t = lax.broadcasted_iota(jnp.int32, (sc, HEAD_DIM), 0)

        def head_body(h, carry_unused):
            def sub_body(c, st):
                rows = pl.ds(pl.multiple_of(c * sc, sc), sc)
                q = q_scr[h, rows, :]
                k = k_scr[h, rows, :]
                iv = i_scr[h, rows, :]
                g = g_scr[h, rows, :]
                gc = g
                for shift in (1, 2, 4):
                    moved = pltpu.roll(gc, shift, 0)
                    gc = gc + jnp.where(r_t >= shift, moved, 0.0)
                g_last = gc[sc - 1:sc, :]
                o = lax.dot_general((q * jnp.exp(gc)).astype(_BF16), st.astype(_BF16),
                                    _NT, preferred_element_type=_F32)
                for s in range(sc):
                    decay = jnp.exp(jnp.minimum(gc - gc[s:s + 1, :], 0.0))
                    pair = jnp.sum(q * k[s:s + 1, :] * decay, axis=-1, keepdims=True)
                    o = o + jnp.where(r_t >= s, pair, 0.0) * iv[s:s + 1, :]
                k_end = (k * jnp.exp(g_last - gc)).astype(_BF16)
                st = jnp.exp(g_last) * st + lax.dot_general(
                    iv.astype(_BF16), k_end, _TN, preferred_element_type=_F32)
                o_scr[h, rows, :] = o
                return st

            st_scr[h] = lax.fori_loop(0, tm // sc, sub_body, st_scr[h])
            finish_head(h, o_scr[h])
            return carry_unused

        lax.fori_loop(0, n_heads, head_body, 0)

    lax.cond(matmul_path_ok, matmul_path, pairwise_path)
    out_ref[0] = _gated_out(x, o_scr, gate_scr, wout_ref, n_heads)


def _layer_call(body, x, operands, scratch_shapes, name):
    batch, seq, d_model = x.shape
    tm = min(ROW_TILE, seq)
    assert seq % tm == 0 and tm % HG_CHUNK == 0

    def whole(a):
        return pl.BlockSpec(a.shape, lambda b, t: (0,) * a.ndim)

    tile = pl.BlockSpec((1, tm, d_model), lambda b, t: (b, t, 0))
    return pl.pallas_call(
        body,
        grid=(batch, seq // tm),
        in_specs=[tile] + [whole(a) for a in operands],
        out_specs=tile,
        out_shape=jax.ShapeDtypeStruct(x.shape, x.dtype),
        scratch_shapes=scratch_shapes(tm, seq),
        compiler_params=pltpu.CompilerParams(
            dimension_semantics=("arbitrary", "arbitrary"),
            vmem_limit_bytes=V7X_VMEM_LIMIT_BYTES),
        name=name,
    )(x, *operands)


def _sb_layer(x, norm_gain, w_in, q_gain, k_gain, w_out):
    width = w_out.shape[0]
    n_heads = width // HEAD_DIM
    assert w_in.shape[1] == 4 * width

    def scratch(tm, seq):
        return [
            pltpu.VMEM((n_heads, tm, HEAD_DIM), _BF16),
            pltpu.VMEM((n_heads, seq, HEAD_DIM), _BF16),
            pltpu.VMEM((n_heads, seq, HEAD_DIM), _BF16),
            pltpu.VMEM((n_heads, tm, HEAD_DIM), _F32),
            pltpu.VMEM((n_heads, tm, HEAD_DIM), _F32),
            pltpu.VMEM((n_heads, tm, HEAD_DIM), _F32),
        ]

    operands = (norm_gain.reshape(1, -1), w_in.astype(_BF16), q_gain.reshape(1, -1),
                k_gain.reshape(1, -1), w_out.astype(_BF16))
    return _layer_call(functools.partial(_sb_layer_kernel, n_heads=n_heads),
                       x, operands, scratch, "stick_breaking_layer")


def _hg_layer(x, norm_gain, w_in, o_gain, w_out, lb_logits, layer):
    width = w_out.shape[0]
    n_heads = width // HEAD_DIM
    assert w_in.shape[1] == 4 * width

    def scratch(tm, seq):
        per_head = lambda dt: pltpu.VMEM((n_heads, tm, HEAD_DIM), dt)
        return [per_head(_F32)] * 6 + [pltpu.VMEM((n_heads, HEAD_DIM, HEAD_DIM), _F32)]

    operands = (norm_gain.reshape(1, -1), w_in.astype(_BF16), o_gain.reshape(1, -1),
                w_out.astype(_BF16), lb_logits)
    return _layer_call(functools.partial(_hg_layer_kernel, n_heads=n_heads, layer=layer),
                       x, operands, scratch, "hgrn2_layer")


def kernel(x, sb_norm, sb_w_in, sb_q_gain, sb_k_gain, sb_w_out,
           hg_norm, hg_w_in, hg_o_gain, hg_w_out, hg_lb_logits):
    depth = hg_lb_logits.shape[0]
    h = x
    for layer in range(depth):
        j = layer // 2
        if layer % 2 == 0:
            h = _sb_layer(h, sb_norm[j], sb_w_in[j], sb_q_gain[j], sb_k_gain[j], sb_w_out[j])
        else:
            h = _hg_layer(h, hg_norm[j], hg_w_in[j], hg_o_gain[j], hg_w_out[j],
                          hg_lb_logits, layer)
    return h
```

```python
import functools

import jax
import jax.numpy as jnp
from jax import lax
from jax.experimental import pallas as pl
from jax.experimental.pallas import tpu as pltpu

HEAD_DIM = 128
RMS_EPS = 1e-6
LOG2_E = 1.4426950408889634
ROW_TILE = 512
SB_BLOCK = 256
HG_CHUNK = 128
HG_SUBCHUNK = 8
HG_MAX_CHUNK_DECAY = 80.0
SB_PIPELINE_SKEW = 2
HG_PIPELINE_SKEW = 3
V7X_VMEM_LIMIT_BYTES = 56 * 1024 * 1024

_F32 = jnp.float32
_BF16 = jnp.bfloat16
_NT = (((1,), (1,)), ((), ()))
_TN = (((0,), (0,)), ((), ()))


def _rms_rows(x, gain):
    return x * lax.rsqrt(jnp.mean(x * x, axis=-1, keepdims=True) + RMS_EPS) * gain


def _silu(x):
    half = 0.5 * x
    return half + half * jnp.tanh(half)


def _project(x, gain, win_ref, width, group_order, store):
    u = _rms_rows(x, gain).astype(_BF16)
    n_heads = width // HEAD_DIM

    def matmul(grp):
        return jnp.dot(u, win_ref[:, grp * width:(grp + 1) * width],
                       preferred_element_type=_F32)

    ahead = matmul(group_order[0])
    for n, grp in enumerate(group_order):
        pg = ahead
        if n + 1 < len(group_order):
            ahead = matmul(group_order[n + 1])
        for h in range(n_heads):
            store(grp, h, pg[:, h * HEAD_DIM:(h + 1) * HEAD_DIM])


def _gated_out(x, o_scr, gate_scr, wout_ref, n_heads):
    o = jnp.concatenate([o_scr[h] * gate_scr[h] for h in range(n_heads)], axis=1)
    return x + jnp.dot(o.astype(_BF16), wout_ref[...], preferred_element_type=_F32)


def _software_pipeline(items, first, second, third, skew):
    a_out, b_out = {}, {}
    for step in range(len(items) + 2 * skew):
        n1, n2, n3 = step, step - skew, step - 2 * skew
        if n1 < len(items):
            a_out[n1] = first(*items[n1])
        if 0 <= n2 < len(items):
            b_out[n2] = second(*items[n2], *a_out.pop(n2))
        if 0 <= n3 < len(items):
            third(*items[n3], *b_out.pop(n3))


def _sb_layer_kernel(x_ref, ng_ref, win_ref, qg_ref, kg_ref, wout_ref, out_ref,
                     q_scr, k_scr, v_scr, gate_scr, acc_scr, sum_scr, *, n_heads):
    tm = x_ref.shape[1]
    blk = SB_BLOCK
    n_sub = tm // blk
    i = pl.program_id(1)
    row0 = pl.multiple_of(i * tm, tm)
    x = x_ref[0]
    qg = qg_ref[...] * (HEAD_DIM ** -0.5 * LOG2_E)
    kg = kg_ref[...]

    def store(grp, h, tile):
        if grp == 0:
            q_scr[h] = _rms_rows(tile, qg).astype(_BF16)
        elif grp == 1:
            k_scr[h, pl.ds(row0, tm), :] = _rms_rows(tile, kg).astype(_BF16)
        elif grp == 2:
            v_scr[h, pl.ds(row0, tm), :] = tile.astype(_BF16)
        else:
            gate_scr[h] = _silu(tile)

    _project(x, ng_ref[...], win_ref, n_heads * HEAD_DIM, (0, 1, 3, 2), store)

    t_idx = lax.broadcasted_iota(jnp.int32, (blk, blk), 0)
    s_idx = lax.broadcasted_iota(jnp.int32, (blk, blk), 1)
    causal = s_idx < t_idx
    from_key = jnp.where(s_idx <= t_idx, 1.0, 0.0).astype(_BF16)


    def gates(h, a, r0, diag):
        q = q_scr[h, a * blk:(a + 1) * blk, :]
        z2 = lax.dot_general(q, k_scr[h, pl.ds(r0, blk), :], _NT, preferred_element_type=_F32)
        skip = jnp.maximum(z2, 0.0) + jnp.log(1.0 + jnp.exp2(-jnp.abs(z2))) * LOG2_E
        if diag:
            skip = jnp.where(causal, skip, 0.0)
        row_sum = jnp.sum(skip, axis=-1, keepdims=True)
        return z2, skip.astype(_BF16), row_sum

    def weights(h, a, r0, diag, z2, skip, row_sum):
        rows = slice(a * blk, (a + 1) * blk)
        skipped = jnp.dot(skip, from_key, preferred_element_type=_F32)
        if diag:
            sum_scr[h, rows, :] = jnp.broadcast_to(row_sum, (blk, HEAD_DIM))
            return (jnp.where(causal, jnp.exp2(z2 - skipped), 0.0).astype(_BF16),)
        nearer = sum_scr[h, rows, :]
        sum_scr[h, rows, :] = nearer + row_sum
        return (jnp.exp2(z2 - skipped
                         - jnp.concatenate([nearer] * (blk // HEAD_DIM), axis=1)).astype(_BF16),)

    def accumulate(h, a, r0, diag, w):
        rows = slice(a * blk, (a + 1) * blk)
        o = jnp.dot(w, v_scr[h, pl.ds(r0, blk), :], preferred_element_type=_F32)
        if diag:
            acc_scr[h, rows, :] = o
        else:
            acc_scr[h, rows, :] += o

    in_tile = [(h, a, row0 + kb * blk, kb == a)
               for a in range(n_sub) for kb in range(a, -1, -1) for h in range(n_heads)]
    _software_pipeline(in_tile, gates, weights, accumulate, SB_PIPELINE_SKEW)

    def kv_body(n, carry_unused):
        r0 = pl.multiple_of(row0 - (n + 1) * blk, blk)
        earlier = [(h, a, r0, False) for a in range(n_sub) for h in range(n_heads)]
        _software_pipeline(earlier, gates, weights, accumulate, SB_PIPELINE_SKEW)
        return carry_unused

    lax.fori_loop(0, n_sub * i, kv_body, 0)
    out_ref[0] = _gated_out(x, acc_scr, gate_scr, wout_ref, n_heads)


def _hg_layer_kernel(x_ref, ng_ref, win_ref, og_ref, wout_ref, lbl_ref, out_ref,
                     q_scr, g_scr, k_scr, i_scr, gate_scr, o_scr, st_scr,
                     *, n_heads, layer):
    tm = x_ref.shape[1]
    n_chunks = tm // HG_CHUNK
    x = x_ref[0]
    og = og_ref[...]

    @pl.when(pl.program_id(1) == 0)
    def _():
        st_scr[...] = jnp.zeros_like(st_scr)

    lg = lbl_ref[...]
    lmax = jnp.max(lg, axis=0, keepdims=True)
    le = jnp.exp(lg - lmax)
    lb_row = jnp.sum(le[1:layer + 1], axis=0, keepdims=True) / jnp.sum(le, axis=0, keepdims=True)

    chunk_min = [None]

    def store(grp, h, tile):
        if grp == 0:
            q_scr[h] = _silu(tile)
        elif grp == 1:
            lb = lb_row[:, h * HEAD_DIM:(h + 1) * HEAD_DIM]
            half_span = 0.5 * (1.0 - lb)
            f = (lb + half_span) + half_span * jnp.tanh(0.5 * tile)
            f = jnp.maximum(f, lb)
            g = jnp.log(f)
            g_scr[h] = g
            k_scr[h] = 1.0 - f
            for c in range(n_chunks):
                tot = jnp.sum(g[c * HG_CHUNK:(c + 1) * HG_CHUNK], axis=0, keepdims=True)
                chunk_min[0] = tot if chunk_min[0] is None else jnp.minimum(chunk_min[0], tot)
        elif grp == 2:
            i_scr[h] = tile
        else:
            gate_scr[h] = _silu(tile)

    _project(x, ng_ref[...], win_ref, n_heads * HEAD_DIM, (1, 0, 3, 2), store)
    matmul_path_ok = jnp.min(chunk_min[0]) >= -HG_MAX_CHUNK_DECAY

    def matmul_path():
        c_t = lax.broadcasted_iota(jnp.int32, (HG_CHUNK, HG_CHUNK), 0)
        c_s = lax.broadcasted_iota(jnp.int32, (HG_CHUNK, HG_CHUNK), 1)
        incl = c_s <= c_t
        earlier_rows = jnp.where(incl, 1.0, 0.0).astype(_BF16)


        def decays(c, h):
            rows = pl.ds(c * HG_CHUNK, HG_CHUNK)
            q = q_scr[h, rows, :]
            k = k_scr[h, rows, :]
            g = g_scr[h, rows, :]
            hi = g.astype(_BF16)
            lo = (g - hi.astype(_F32)).astype(_BF16)
            parts = jnp.dot(earlier_rows, jnp.concatenate([hi, lo], axis=1),
                            preferred_element_type=_F32)
            gc = parts[:, :HEAD_DIM] + parts[:, HEAD_DIM:]
            g_last = gc[HG_CHUNK - 1:HG_CHUNK, :]
            g_mid = gc[HG_CHUNK // 2 - 1:HG_CHUNK // 2, :]
            to_mid = jnp.exp(gc - g_mid)
            q_mid = q * to_mid
            k_mid = k / to_mid
            q_in = q_mid * jnp.exp(g_mid)
            k_end = k_mid * jnp.exp(g_last - g_mid)
            return (q_mid.astype(_BF16), k_mid.astype(_BF16), q_in.astype(_BF16),
                    k_end.astype(_BF16), jnp.exp(g_last))

        def intra_scores(c, h, q_mid, k_mid, q_in, k_end, decay_last):
            scores = lax.dot_general(q_mid, k_mid, _NT, preferred_element_type=_F32)
            return jnp.where(incl, scores, 0.0).astype(_BF16), q_in, k_end, decay_last

        def outputs(c, h, scores, q_in, k_end, decay_last):
            rows = pl.ds(c * HG_CHUNK, HG_CHUNK)
            iv = i_scr[h, rows, :].astype(_BF16)
            st = st_scr[h]
            o = jnp.dot(scores, iv, preferred_element_type=_F32)
            o = o + lax.dot_general(q_in, st.astype(_BF16), _NT, preferred_element_type=_F32)
            st_scr[h] = decay_last * st + lax.dot_general(
                iv, k_end, _TN, preferred_element_type=_F32)
            o_scr[h, rows, :] = _rms_rows(o, og)

        items = [(c, h) for c in range(n_chunks) for h in range(n_heads)]
        _software_pipeline(items, decays, intra_scores, outputs, HG_PIPELINE_SKEW)

    def pairwise_path():
        sc = HG_SUBCHUNK
        r_t = lax.broadcasted_iota(jnp.int32, (sc, HEAD_DIM), 0)

        def head_body(h, carry_unused):
            def sub_body(c, st):
                rows = pl.ds(pl.multiple_of(c * sc, sc), sc)
                q = q_scr[h, rows, :]
                k = k_scr[h, rows, :]
                iv = i_scr[h, rows, :]
                g = g_scr[h, rows, :]
                gc = g
                for shift in (1, 2, 4):
                    moved = pltpu.roll(gc, shift, 0)
                    gc = gc + jnp.where(r_t >= shift, moved, 0.0)
                g_last = gc[sc - 1:sc, :]
                o = lax.dot_general((q * jnp.exp(gc)).astype(_BF16), st.astype(_BF16),
                                    _NT, preferred_element_type=_F32)
                for s in range(sc):
                    decay = jnp.exp(jnp.minimum(gc - gc[s:s + 1, :], 0.0))
                    pair = jnp.sum(q * k[s:s + 1, :] * decay, axis=-1, keepdims=True)
                    o = o + jnp.where(r_t >= s, pair, 0.0) * iv[s:s + 1, :]
                k_end = (k * jnp.exp(g_last - gc)).astype(_BF16)
                st = jnp.exp(g_last) * st + lax.dot_general(
                    iv.astype(_BF16), k_end, _TN, preferred_element_type=_F32)
                o_scr[h, rows, :] = _rms_rows(o, og)
                return st

            st_scr[h] = lax.fori_loop(0, tm // sc, sub_body, st_scr[h])
            return carry_unused

        lax.fori_loop(0, n_heads, head_body, 0)

    lax.cond(matmul_path_ok, matmul_path, pairwise_path)
    out_ref[0] = _gated_out(x, o_scr, gate_scr, wout_ref, n_heads)


def _layer_call(body, x, operands, scratch_shapes, name):
    batch, seq, d_model = x.shape
    tm = min(ROW_TILE, seq)
    assert seq % tm == 0 and tm % SB_BLOCK == 0 and tm % HG_CHUNK == 0

    def whole(a):
        return pl.BlockSpec(a.shape, lambda b, t: (0,) * a.ndim)

    tile = pl.BlockSpec((1, tm, d_model), lambda b, t: (b, t, 0))
    return pl.pallas_call(
        body,
        grid=(batch, seq // tm),
        in_specs=[tile] + [whole(a) for a in operands],
        out_specs=tile,
        out_shape=jax.ShapeDtypeStruct(x.shape, x.dtype),
        scratch_shapes=scratch_shapes(tm, seq),
        compiler_params=pltpu.CompilerParams(
            dimension_semantics=("arbitrary", "arbitrary"),
            vmem_limit_bytes=V7X_VMEM_LIMIT_BYTES),
        name=name,
    )(x, *operands)


def _sb_layer(x, norm_gain, w_in, q_gain, k_gain, w_out):
    width = w_out.shape[0]
    n_heads = width // HEAD_DIM
    assert w_in.shape[1] == 4 * width

    def scratch(tm, seq):
        return [
            pltpu.VMEM((n_heads, tm, HEAD_DIM), _BF16),
            pltpu.VMEM((n_heads, seq, HEAD_DIM), _BF16),
            pltpu.VMEM((n_heads, seq, HEAD_DIM), _BF16),
            pltpu.VMEM((n_heads, tm, HEAD_DIM), _F32),
            pltpu.VMEM((n_heads, tm, HEAD_DIM), _F32),
            pltpu.VMEM((n_heads, tm, HEAD_DIM), _F32),
        ]

    operands = (norm_gain.reshape(1, -1), w_in.astype(_BF16), q_gain.reshape(1, -1),
                k_gain.reshape(1, -1), w_out.astype(_BF16))
    return _layer_call(functools.partial(_sb_layer_kernel, n_heads=n_heads),
                       x, operands, scratch, "stick_breaking_layer")


def _hg_layer(x, norm_gain, w_in, o_gain, w_out, lb_logits, layer):
    width = w_out.shape[0]
    n_heads = width // HEAD_DIM
    assert w_in.shape[1] == 4 * width

    def scratch(tm, seq):
        per_head = lambda dt: pltpu.VMEM((n_heads, tm, HEAD_DIM), dt)
        return [per_head(_F32)] * 6 + [pltpu.VMEM((n_heads, HEAD_DIM, HEAD_DIM), _F32)]

    operands = (norm_gain.reshape(1, -1), w_in.astype(_BF16), o_gain.reshape(1, -1),
                w_out.astype(_BF16), lb_logits)
    return _layer_call(functools.partial(_hg_layer_kernel, n_heads=n_heads, layer=layer),
                       x, operands, scratch, "hgrn2_layer")


def kernel(x, sb_norm, sb_w_in, sb_q_gain, sb_k_gain, sb_w_out,
           hg_norm, hg_w_in, hg_o_gain, hg_w_out, hg_lb_logits):
    depth = hg_lb_logits.shape[0]
    h = x
    for layer in range(depth):
        j = layer // 2
        if layer % 2 == 0:
            h = _sb_layer(h, sb_norm[j], sb_w_in[j], sb_q_gain[j], sb_k_gain[j], sb_w_out[j])
        else:
            h = _hg_layer(h, hg_norm[j], hg_w_in[j], hg_o_gain[j], hg_w_out[j],
                          hg_lb_logits, layer)
    return h
```

```python
import functools

import jax
import jax.numpy as jnp
from jax import lax
from jax.experimental import pallas as pl
from jax.experimental.pallas import tpu as pltpu

HEAD_DIM = 128
RMS_EPS = 1e-6
LOG2_E = 1.4426950408889634
ROW_TILE = 512
SB_BLOCK = 256
SB_SOFTPLUS_CAP = 64.0
SB_DEAD_SKIP_MASS = 160.0
HG_CHUNK = 128
HG_SUBCHUNK = 8
HG_MAX_CHUNK_DECAY = 80.0
SB_PIPELINE_SKEW = 2
HG_PIPELINE_SKEW = 3
V7X_VMEM_LIMIT_BYTES = 56 * 1024 * 1024

_F32 = jnp.float32
_BF16 = jnp.bfloat16
_NT = (((1,), (1,)), ((), ()))
_TN = (((0,), (0,)), ((), ()))


def _rms_rows(x, gain):
    return x * lax.rsqrt(jnp.mean(x * x, axis=-1, keepdims=True) + RMS_EPS) * gain


def _silu(x):
    half = 0.5 * x
    return half + half * jnp.tanh(half)


def _project(x, gain, win_ref, width, group_order, store):
    u = _rms_rows(x, gain).astype(_BF16)
    n_heads = width // HEAD_DIM

    def matmul(grp):
        return jnp.dot(u, win_ref[:, grp * width:(grp + 1) * width],
                       preferred_element_type=_F32)

    ahead = matmul(group_order[0])
    for n, grp in enumerate(group_order):
        pg = ahead
        if n + 1 < len(group_order):
            ahead = matmul(group_order[n + 1])
        for h in range(n_heads):
            store(grp, h, pg[:, h * HEAD_DIM:(h + 1) * HEAD_DIM])


def _gated_out(x, o_scr, gate_scr, wout_ref, n_heads):
    o = jnp.concatenate([o_scr[h] * gate_scr[h] for h in range(n_heads)], axis=1)
    return x + jnp.dot(o.astype(_BF16), wout_ref[...], preferred_element_type=_F32)


def _software_pipeline(items, first, second, third, skew):
    a_out, b_out = {}, {}
    for step in range(len(items) + 2 * skew):
        n1, n2, n3 = step, step - skew, step - 2 * skew
        if n1 < len(items):
            a_out[n1] = first(*items[n1])
        if 0 <= n2 < len(items):
            b_out[n2] = second(*items[n2], *a_out.pop(n2))
        if 0 <= n3 < len(items):
            third(*items[n3], *b_out.pop(n3))


def _sb_layer_kernel(x_ref, ng_ref, win_ref, qg_ref, kg_ref, wout_ref, out_ref,
                     q_scr, k_scr, v_scr, gate_scr, acc_scr, sum_scr, *, n_heads):
    tm = x_ref.shape[1]
    blk = SB_BLOCK
    n_sub = tm // blk
    i = pl.program_id(1)
    row0 = pl.multiple_of(i * tm, tm)
    x = x_ref[0]
    qg = qg_ref[...] * (HEAD_DIM ** -0.5 * LOG2_E)
    kg = kg_ref[...]

    def store(grp, h, tile):
        if grp == 0:
            q_scr[h] = _rms_rows(tile, qg).astype(_BF16)
        elif grp == 1:
            k_scr[h, pl.ds(row0, tm), :] = _rms_rows(tile, kg).astype(_BF16)
        elif grp == 2:
            v_scr[h, pl.ds(row0, tm), :] = tile.astype(_BF16)
        else:
            gate_scr[h] = _silu(tile)

    _project(x, ng_ref[...], win_ref, n_heads * HEAD_DIM, (0, 1, 3, 2), store)

    t_idx = lax.broadcasted_iota(jnp.int32, (blk, blk), 0)
    s_idx = lax.broadcasted_iota(jnp.int32, (blk, blk), 1)
    causal = s_idx < t_idx
    later_keys = jnp.where(causal, 1.0, 0.0).astype(_BF16)


    def gates(h, a, r0, diag):
        q = q_scr[h, a * blk:(a + 1) * blk, :]
        z2 = lax.dot_general(q, k_scr[h, pl.ds(r0, blk), :], _NT, preferred_element_type=_F32)
        skip = jnp.maximum(
            z2, jnp.log(1.0 + jnp.exp2(jnp.minimum(z2, SB_SOFTPLUS_CAP))) * LOG2_E)
        log_beta = z2 - skip
        if diag:
            skip = jnp.where(causal, skip, 0.0)
        row_sum = jnp.sum(skip, axis=-1, keepdims=True)
        return log_beta, skip.astype(_BF16), row_sum

    def weights(h, a, r0, diag, log_beta, skip, row_sum):
        rows = slice(a * blk, (a + 1) * blk)
        skipped = jnp.dot(skip, later_keys, preferred_element_type=_F32)
        if diag:
            sum_scr[h, rows, :] = jnp.broadcast_to(row_sum, (blk, HEAD_DIM))
            return (jnp.where(causal, jnp.exp2(log_beta - skipped), 0.0).astype(_BF16),)
        nearer = sum_scr[h, rows, :]
        sum_scr[h, rows, :] = nearer + row_sum
        return (jnp.exp2(log_beta - skipped
                         - jnp.concatenate([nearer] * (blk // HEAD_DIM), axis=1)).astype(_BF16),)

    def accumulate(h, a, r0, diag, w):
        rows = slice(a * blk, (a + 1) * blk)
        o = jnp.dot(w, v_scr[h, pl.ds(r0, blk), :], preferred_element_type=_F32)
        if diag:
            acc_scr[h, rows, :] = o
        else:
            acc_scr[h, rows, :] += o

    in_tile = [(h, a, row0 + kb * blk, kb == a)
               for a in range(n_sub) for kb in range(a, -1, -1) for h in range(n_heads)]
    _software_pipeline(in_tile, gates, weights, accumulate, SB_PIPELINE_SKEW)


    def least_skip_mass(a):
        rows = slice(a * blk, (a + 1) * blk)
        least = sum_scr[0, rows, :]
        for h in range(1, n_heads):
            least = jnp.minimum(least, sum_scr[h, rows, :])
        return jnp.min(least)

    for a in range(n_sub):
        def kv_cond(state):
            n, least = state
            return jnp.logical_and(n < n_sub * i, least < SB_DEAD_SKIP_MASS)

        def kv_body(state, a=a):
            n, _ = state
            r0 = pl.multiple_of(row0 - (n + 1) * blk, blk)
            earlier = [(h, a, r0, False) for h in range(n_heads)]
            _software_pipeline(earlier, gates, weights, accumulate, SB_PIPELINE_SKEW)
            return n + 1, least_skip_mass(a)

        lax.while_loop(kv_cond, kv_body, (jnp.int32(0), least_skip_mass(a)))

    out_ref[0] = _gated_out(x, acc_scr, gate_scr, wout_ref, n_heads)


def _hg_layer_kernel(x_ref, ng_ref, win_ref, og_ref, wout_ref, lbl_ref, out_ref,
                     q_scr, g_scr, k_scr, i_scr, gate_scr, o_scr, st_scr,
                     *, n_heads, layer):
    tm = x_ref.shape[1]
    n_chunks = tm // HG_CHUNK
    x = x_ref[0]
    og = og_ref[...]

    @pl.when(pl.program_id(1) == 0)
    def _():
        st_scr[...] = jnp.zeros_like(st_scr)

    lg = lbl_ref[...]
    lmax = jnp.max(lg, axis=0, keepdims=True)
    le = jnp.exp(lg - lmax)
    lb_row = jnp.sum(le[1:layer + 1], axis=0, keepdims=True) / jnp.sum(le, axis=0, keepdims=True)

    chunk_min = [None]

    def store(grp, h, tile):
        if grp == 0:
            q_scr[h] = _silu(tile)
        elif grp == 1:
            lb = lb_row[:, h * HEAD_DIM:(h + 1) * HEAD_DIM]
            half_span = 0.5 * (1.0 - lb)
            f = (lb + half_span) + half_span * jnp.tanh(0.5 * tile)
            f = jnp.maximum(f, lb)
            g = jnp.log(f)
            g_scr[h] = g
            k_scr[h] = 1.0 - f
            for c in range(n_chunks):
                tot = jnp.sum(g[c * HG_CHUNK:(c + 1) * HG_CHUNK], axis=0, keepdims=True)
                chunk_min[0] = tot if chunk_min[0] is None else jnp.minimum(chunk_min[0], tot)
        elif grp == 2:
            i_scr[h] = tile
        else:
            gate_scr[h] = _silu(tile)

    _project(x, ng_ref[...], win_ref, n_heads * HEAD_DIM, (1, 0, 3, 2), store)
    matmul_path_ok = jnp.min(chunk_min[0]) >= -HG_MAX_CHUNK_DECAY

    def matmul_path():
        c_t = lax.broadcasted_iota(jnp.int32, (HG_CHUNK, HG_CHUNK), 0)
        c_s = lax.broadcasted_iota(jnp.int32, (HG_CHUNK, HG_CHUNK), 1)
        incl = c_s <= c_t
        earlier_rows = jnp.where(incl, 1.0, 0.0).astype(_BF16)


        def decays(c, h):
            rows = pl.ds(c * HG_CHUNK, HG_CHUNK)
            q = q_scr[h, rows, :]
            k = k_scr[h, rows, :]
            g = g_scr[h, rows, :]
            hi = g.astype(_BF16)
            lo = (g - hi.astype(_F32)).astype(_BF16)
            parts = jnp.dot(earlier_rows, jnp.concatenate([hi, lo], axis=1),
                            preferred_element_type=_F32)
            gc = parts[:, :HEAD_DIM] + parts[:, HEAD_DIM:]
            g_last = gc[HG_CHUNK - 1:HG_CHUNK, :]
            g_mid = gc[HG_CHUNK // 2 - 1:HG_CHUNK // 2, :]
            to_mid = jnp.exp(gc - g_mid)
            q_mid = q * to_mid
            k_mid = k / to_mid
            q_in = q_mid * jnp.exp(g_mid)
            k_end = k_mid * jnp.exp(g_last - g_mid)
            return (q_mid.astype(_BF16), k_mid.astype(_BF16), q_in.astype(_BF16),
                    k_end.astype(_BF16), jnp.exp(g_last))

        def intra_scores(c, h, q_mid, k_mid, q_in, k_end, decay_last):
            scores = lax.dot_general(q_mid, k_mid, _NT, preferred_element_type=_F32)
            return jnp.where(incl, scores, 0.0).astype(_BF16), q_in, k_end, decay_last

        def outputs(c, h, scores, q_in, k_end, decay_last):
            rows = pl.ds(c * HG_CHUNK, HG_CHUNK)
            iv = i_scr[h, rows, :].astype(_BF16)
            st = st_scr[h]
            o = jnp.dot(scores, iv, preferred_element_type=_F32)
            o = o + lax.dot_general(q_in, st.astype(_BF16), _NT, preferred_element_type=_F32)
            st_scr[h] = decay_last * st + lax.dot_general(
                iv, k_end, _TN, preferred_element_type=_F32)
            o_scr[h, rows, :] = _rms_rows(o, og)

        items = [(c, h) for c in range(n_chunks) for h in range(n_heads)]
        _software_pipeline(items, decays, intra_scores, outputs, HG_PIPELINE_SKEW)

    def pairwise_path():
        sc = HG_SUBCHUNK
        r_t = lax.broadcasted_iota(jnp.int32, (sc, HEAD_DIM), 0)

        def head_body(h, carry_unused):
            def sub_body(c, st):
                rows = pl.ds(pl.multiple_of(c * sc, sc), sc)
                q = q_scr[h, rows, :]
                k = k_scr[h, rows, :]
                iv = i_scr[h, rows, :]
                g = g_scr[h, rows, :]
                gc = g
                for shift in (1, 2, 4):
                    moved = pltpu.roll(gc, shift, 0)
                    gc = gc + jnp.where(r_t >= shift, moved, 0.0)
                g_last = gc[sc - 1:sc, :]
                o = lax.dot_general((q * jnp.exp(gc)).astype(_BF16), st.astype(_BF16),
                                    _NT, preferred_element_type=_F32)
                for s in range(sc):
                    decay = jnp.exp(jnp.minimum(gc - gc[s:s + 1, :], 0.0))
                    pair = jnp.sum(q * k[s:s + 1, :] * decay, axis=-1, keepdims=True)
                    o = o + jnp.where(r_t >= s, pair, 0.0) * iv[s:s + 1, :]
                k_end = (k * jnp.exp(g_last - gc)).astype(_BF16)
                st = jnp.exp(g_last) * st + lax.dot_general(
                    iv.astype(_BF16), k_end, _TN, preferred_element_type=_F32)
                o_scr[h, rows, :] = _rms_rows(o, og)
                return st

            st_scr[h] = lax.fori_loop(0, tm // sc, sub_body, st_scr[h])
            return carry_unused

        lax.fori_loop(0, n_heads, head_body, 0)

    lax.cond(matmul_path_ok, matmul_path, pairwise_path)
    out_ref[0] = _gated_out(x, o_scr, gate_scr, wout_ref, n_heads)


def _layer_call(body, x, operands, scratch_shapes, name):
    batch, seq, d_model = x.shape
    tm = min(ROW_TILE, seq)
    assert seq % tm == 0 and tm % SB_BLOCK == 0 and tm % HG_CHUNK == 0

    def whole(a):
        return pl.BlockSpec(a.shape, lambda b, t: (0,) * a.ndim)

    tile = pl.BlockSpec((1, tm, d_model), lambda b, t: (b, t, 0))
    return pl.pallas_call(
        body,
        grid=(batch, seq // tm),
        in_specs=[tile] + [whole(a) for a in operands],
        out_specs=tile,
        out_shape=jax.ShapeDtypeStruct(x.shape, x.dtype),
        scratch_shapes=scratch_shapes(tm, seq),
        compiler_params=pltpu.CompilerParams(
            dimension_semantics=("arbitrary", "arbitrary"),
            vmem_limit_bytes=V7X_VMEM_LIMIT_BYTES),
        name=name,
    )(x, *operands)


def _sb_layer(x, norm_gain, w_in, q_gain, k_gain, w_out):
    width = w_out.shape[0]
    n_heads = width // HEAD_DIM
    assert w_in.shape[1] == 4 * width

    def scratch(tm, seq):
        return [
            pltpu.VMEM((n_heads, tm, HEAD_DIM), _BF16),
            pltpu.VMEM((n_heads, seq, HEAD_DIM), _BF16),
            pltpu.VMEM((n_heads, seq, HEAD_DIM), _BF16),
            pltpu.VMEM((n_heads, tm, HEAD_DIM), _F32),
            pltpu.VMEM((n_heads, tm, HEAD_DIM), _F32),
            pltpu.VMEM((n_heads, tm, HEAD_DIM), _F32),
        ]

    operands = (norm_gain.reshape(1, -1), w_in.astype(_BF16), q_gain.reshape(1, -1),
                k_gain.reshape(1, -1), w_out.astype(_BF16))
    return _layer_call(functools.partial(_sb_layer_kernel, n_heads=n_heads),
                       x, operands, scratch, "stick_breaking_layer")


def _hg_layer(x, norm_gain, w_in, o_gain, w_out, lb_logits, layer):
    width = w_out.shape[0]
    n_heads = width // HEAD_DIM
    assert w_in.shape[1] == 4 * width

    def scratch(tm, seq):
        per_head = lambda dt: pltpu.VMEM((n_heads, tm, HEAD_DIM), dt)
        return [per_head(_F32)] * 6 + [pltpu.VMEM((n_heads, HEAD_DIM, HEAD_DIM), _F32)]

    operands = (norm_gain.reshape(1, -1), w_in.astype(_BF16), o_gain.reshape(1, -1),
                w_out.astype(_BF16), lb_logits)
    return _layer_call(functools.partial(_hg_layer_kernel, n_heads=n_heads, layer=layer),
                       x, operands, scratch, "hgrn2_layer")


def kernel(x, sb_norm, sb_w_in, sb_q_gain, sb_k_gain, sb_w_out,
           hg_norm, hg_w_in, hg_o_gain, hg_w_out, hg_lb_logits):
    depth = hg_lb_logits.shape[0]
    h = x
    for layer in range(depth):
        j = layer // 2
        if layer % 2 == 0:
            h = _sb_layer(h, sb_norm[j], sb_w_in[j], sb_q_gain[j], sb_k_gain[j], sb_w_out[j])
        else:
            h = _hg_layer(h, hg_norm[j], hg_w_in[j], hg_o_gain[j], hg_w_out[j],
                          hg_lb_logits, layer)
    return h
```

```python
import functools

import jax
import jax.numpy as jnp
from jax import lax
from jax.experimental import pallas as pl
from jax.experimental.pallas import tpu as pltpu

HEAD_DIM = 128
RMS_EPS = 1e-6
LOG2_E = 1.4426950408889634
ROW_TILE = 512
NORM_ROWS = 32
SB_BLOCK = 256
SB_SOFTPLUS_CAP = 64.0
SB_DEAD_SKIP_MASS = 160.0
HG_CHUNK = 128
HG_SUBCHUNK = 8
HG_MAX_CHUNK_DECAY = 80.0
SB_PIPELINE_SKEW = 2
HG_PIPELINE_SKEW = 3
V7X_VMEM_LIMIT_BYTES = 56 * 1024 * 1024

_F32 = jnp.float32
_BF16 = jnp.bfloat16
_NT = (((1,), (1,)), ((), ()))
_TN = (((0,), (0,)), ((), ()))


def _rms_rows(x, gain):
    return x * lax.rsqrt(jnp.mean(x * x, axis=-1, keepdims=True) + RMS_EPS) * gain


def _silu(x):
    half = 0.5 * x
    return half + half * jnp.tanh(half)


def _project(x_ref, gain, win_ref, u_scr, width, group_order, store):
    for r in range(0, u_scr.shape[0], NORM_ROWS):
        u_scr[r:r + NORM_ROWS, :] = _rms_rows(x_ref[0, r:r + NORM_ROWS, :], gain).astype(_BF16)
    u = u_scr[...]
    n_heads = width // HEAD_DIM

    def matmul(grp):
        return jnp.dot(u, win_ref[:, grp * width:(grp + 1) * width],
                       preferred_element_type=_F32)

    ahead = matmul(group_order[0])
    for n, grp in enumerate(group_order):
        pg = ahead
        if n + 1 < len(group_order):
            ahead = matmul(group_order[n + 1])
        for h in range(n_heads):
            store(grp, h, pg[:, h * HEAD_DIM:(h + 1) * HEAD_DIM])


def _gated_out(x_ref, out_ref, o_scr, gate_scr, wout_ref, rows):
    o = jnp.concatenate([o_scr[h, rows, :] * gate_scr[h, rows, :]
                         for h in range(o_scr.shape[0])], axis=1)
    out_ref[0, rows, :] = x_ref[0, rows, :] + jnp.dot(
        o.astype(_BF16), wout_ref[...], preferred_element_type=_F32)


def _software_pipeline(items, first, second, third, skew):
    a_out, b_out = {}, {}
    for step in range(len(items) + 2 * skew):
        n1, n2, n3 = step, step - skew, step - 2 * skew
        if n1 < len(items):
            a_out[n1] = first(*items[n1])
        if 0 <= n2 < len(items):
            b_out[n2] = second(*items[n2], *a_out.pop(n2))
        if 0 <= n3 < len(items):
            third(*items[n3], *b_out.pop(n3))


def _sb_layer_kernel(x_ref, ng_ref, win_ref, qg_ref, kg_ref, wout_ref, out_ref,
                     u_scr, q_scr, k_scr, v_scr, gate_scr, acc_scr, sum_scr, *, n_heads):
    tm = x_ref.shape[1]
    blk = SB_BLOCK
    n_sub = tm // blk
    i = pl.program_id(1)
    row0 = pl.multiple_of(i * tm, tm)
    qg = qg_ref[...] * (HEAD_DIM ** -0.5 * LOG2_E)
    kg = kg_ref[...]

    def store(grp, h, tile):
        if grp == 0:
            q_scr[h] = _rms_rows(tile, qg).astype(_BF16)
        elif grp == 1:
            k_scr[h, pl.ds(row0, tm), :] = _rms_rows(tile, kg).astype(_BF16)
        elif grp == 2:
            v_scr[h, pl.ds(row0, tm), :] = tile.astype(_BF16)
        else:
            gate_scr[h] = _silu(tile)

    _project(x_ref, ng_ref[...], win_ref, u_scr, n_heads * HEAD_DIM, (0, 1, 3, 2), store)

    t_idx = lax.broadcasted_iota(jnp.int32, (blk, blk), 0)
    s_idx = lax.broadcasted_iota(jnp.int32, (blk, blk), 1)
    causal = s_idx < t_idx
    later_keys = jnp.where(causal, 1.0, 0.0).astype(_BF16)


    def gates(h, a, r0, diag):
        q = q_scr[h, a * blk:(a + 1) * blk, :]
        z2 = lax.dot_general(q, k_scr[h, pl.ds(r0, blk), :], _NT, preferred_element_type=_F32)
        skip = jnp.maximum(
            z2, jnp.log(1.0 + jnp.exp2(jnp.minimum(z2, SB_SOFTPLUS_CAP))) * LOG2_E)
        log_beta = z2 - skip
        if diag:
            skip = jnp.where(causal, skip, 0.0)
        row_sum = jnp.sum(skip, axis=-1, keepdims=True)
        return log_beta, skip.astype(_BF16), row_sum

    def weights(h, a, r0, diag, log_beta, skip, row_sum):
        rows = slice(a * blk, (a + 1) * blk)
        skipped = jnp.dot(skip, later_keys, preferred_element_type=_F32)
        if diag:
            sum_scr[h, rows, :] = jnp.broadcast_to(row_sum, (blk, HEAD_DIM))
            return (jnp.where(causal, jnp.exp2(log_beta - skipped), 0.0).astype(_BF16),)
        nearer = sum_scr[h, rows, :]
        sum_scr[h, rows, :] = nearer + row_sum
        return (jnp.exp2(log_beta - skipped
                         - jnp.concatenate([nearer] * (blk // HEAD_DIM), axis=1)).astype(_BF16),)

    def accumulate(h, a, r0, diag, w):
        rows = slice(a * blk, (a + 1) * blk)
        o = jnp.dot(w, v_scr[h, pl.ds(r0, blk), :], preferred_element_type=_F32)
        if diag:
            acc_scr[h, rows, :] = o
        else:
            acc_scr[h, rows, :] += o

    in_tile = [(h, a, row0 + kb * blk, kb == a)
               for a in range(n_sub) for kb in range(a, -1, -1) for h in range(n_heads)]
    _software_pipeline(in_tile, gates, weights, accumulate, SB_PIPELINE_SKEW)


    def least_skip_mass(a):
        rows = slice(a * blk, (a + 1) * blk)
        least = sum_scr[0, rows, :]
        for h in range(1, n_heads):
            least = jnp.minimum(least, sum_scr[h, rows, :])
        return jnp.min(least)

    for a in range(n_sub):
        def kv_cond(state):
            n, least = state
            return jnp.logical_and(n < n_sub * i, least < SB_DEAD_SKIP_MASS)

        def kv_body(state, a=a):
            n, _ = state
            r0 = pl.multiple_of(row0 - (n + 1) * blk, blk)
            earlier = [(h, a, r0, False) for h in range(n_heads)]
            _software_pipeline(earlier, gates, weights, accumulate, SB_PIPELINE_SKEW)
            return n + 1, least_skip_mass(a)

        least = least_skip_mass(a) if a > 0 else jnp.float32(0.0)
        lax.while_loop(kv_cond, kv_body, (jnp.int32(0), least))

    _gated_out(x_ref, out_ref, acc_scr, gate_scr, wout_ref, slice(0, tm))


def _hg_layer_kernel(x_ref, ng_ref, win_ref, og_ref, wout_ref, lbl_ref, out_ref,
                     u_scr, q_scr, g_scr, k_scr, i_scr, gate_scr, o_scr, st_scr,
                     *, n_heads, layer):
    tm = x_ref.shape[1]
    n_chunks = tm // HG_CHUNK
    og = og_ref[...]

    @pl.when(pl.program_id(1) == 0)
    def _():
        st_scr[...] = jnp.zeros_like(st_scr)

    lg = lbl_ref[...]
    lmax = jnp.max(lg, axis=0, keepdims=True)
    le = jnp.exp(lg - lmax)
    lb_row = jnp.sum(le[1:layer + 1], axis=0, keepdims=True) / jnp.sum(le, axis=0, keepdims=True)

    chunk_min = [None]

    def store(grp, h, tile):
        if grp == 0:
            q_scr[h] = _silu(tile)
        elif grp == 1:
            lb = lb_row[:, h * HEAD_DIM:(h + 1) * HEAD_DIM]
            half_span = 0.5 * (1.0 - lb)
            f = (lb + half_span) + half_span * jnp.tanh(0.5 * tile)
            f = jnp.maximum(f, lb)
            g = jnp.log(f)
            g_scr[h] = g
            k_scr[h] = 1.0 - f
            for c in range(n_chunks):
                tot = jnp.sum(g[c * HG_CHUNK:(c + 1) * HG_CHUNK], axis=0, keepdims=True)
                chunk_min[0] = tot if chunk_min[0] is None else jnp.minimum(chunk_min[0], tot)
        elif grp == 2:
            i_scr[h] = tile
        else:
            gate_scr[h] = _silu(tile)

    _project(x_ref, ng_ref[...], win_ref, u_scr, n_heads * HEAD_DIM, (1, 0, 3, 2), store)
    matmul_path_ok = jnp.min(chunk_min[0]) >= -HG_MAX_CHUNK_DECAY

    def matmul_path():
        c_t = lax.broadcasted_iota(jnp.int32, (HG_CHUNK, HG_CHUNK), 0)
        c_s = lax.broadcasted_iota(jnp.int32, (HG_CHUNK, HG_CHUNK), 1)
        incl = c_s <= c_t
        earlier_rows = jnp.where(incl, 1.0, 0.0).astype(_BF16)


        def decays(c, h):
            rows = pl.ds(c * HG_CHUNK, HG_CHUNK)
            q = q_scr[h, rows, :]
            k = k_scr[h, rows, :]
            g = g_scr[h, rows, :]
            gc = jnp.dot(earlier_rows, g.astype(_BF16), preferred_element_type=_F32)
            g_last = gc[HG_CHUNK - 1:HG_CHUNK, :]
            g_mid = gc[HG_CHUNK // 2 - 1:HG_CHUNK // 2, :]
            to_mid = jnp.exp(gc - g_mid)
            q_mid = q * to_mid
            k_mid = k / to_mid
            q_in = q_mid * jnp.exp(g_mid)
            k_end = k_mid * jnp.exp(g_last - g_mid)
            return (q_mid.astype(_BF16), k_mid.astype(_BF16), q_in.astype(_BF16),
                    k_end.astype(_BF16), jnp.exp(g_last))

        def intra_scores(c, h, q_mid, k_mid, q_in, k_end, decay_last):
            scores = lax.dot_general(q_mid, k_mid, _NT, preferred_element_type=_F32)
            return jnp.where(incl, scores, 0.0).astype(_BF16), q_in, k_end, decay_last

        def outputs(c, h, scores, q_in, k_end, decay_last):
            rows = pl.ds(c * HG_CHUNK, HG_CHUNK)
            iv = i_scr[h, rows, :].astype(_BF16)
            st = st_scr[h]
            o = jnp.dot(scores, iv, preferred_element_type=_F32)
            o = o + lax.dot_general(q_in, st.astype(_BF16), _NT, preferred_element_type=_F32)
            st_scr[h] = decay_last * st + lax.dot_general(
                iv, k_end, _TN, preferred_element_type=_F32)
            o_scr[h, rows, :] = _rms_rows(o, og)

        items = [(c, h) for c in range(n_chunks) for h in range(n_heads)]
        _software_pipeline(items, decays, intra_scores, outputs, HG_PIPELINE_SKEW)

    def pairwise_path():
        sc = HG_SUBCHUNK
        r_t = lax.broadcasted_iota(jnp.int32, (sc, HEAD_DIM), 0)

        def head_body(h, carry_unused):
            def sub_body(c, st):
                rows = pl.ds(pl.multiple_of(c * sc, sc), sc)
                q = q_scr[h, rows, :]
                k = k_scr[h, rows, :]
                iv = i_scr[h, rows, :]
                g = g_scr[h, rows, :]
                gc = g
                for shift in (1, 2, 4):
                    moved = pltpu.roll(gc, shift, 0)
                    gc = gc + jnp.where(r_t >= shift, moved, 0.0)
                g_last = gc[sc - 1:sc, :]
                o = lax.dot_general((q * jnp.exp(gc)).astype(_BF16), st.astype(_BF16),
                                    _NT, preferred_element_type=_F32)
                for s in range(sc):
                    decay = jnp.exp(jnp.minimum(gc - gc[s:s + 1, :], 0.0))
                    pair = jnp.sum(q * k[s:s + 1, :] * decay, axis=-1, keepdims=True)
                    o = o + jnp.where(r_t >= s, pair, 0.0) * iv[s:s + 1, :]
                k_end = (k * jnp.exp(g_last - gc)).astype(_BF16)
                st = jnp.exp(g_last) * st + lax.dot_general(
                    iv.astype(_BF16), k_end, _TN, preferred_element_type=_F32)
                o_scr[h, rows, :] = _rms_rows(o, og)
                return st

            st_scr[h] = lax.fori_loop(0, tm // sc, sub_body, st_scr[h])
            return carry_unused

        lax.fori_loop(0, n_heads, head_body, 0)

    lax.cond(matmul_path_ok, matmul_path, pairwise_path)
    _gated_out(x_ref, out_ref, o_scr, gate_scr, wout_ref, slice(0, tm))


def _layer_call(body, x, operands, scratch_shapes, name):
    batch, seq, d_model = x.shape
    tm = min(ROW_TILE, seq)
    assert seq % tm == 0 and tm % SB_BLOCK == 0 and tm % HG_CHUNK == 0

    def whole(a):
        return pl.BlockSpec(a.shape, lambda b, t: (0,) * a.ndim)

    tile = pl.BlockSpec((1, tm, d_model), lambda b, t: (b, t, 0))
    return pl.pallas_call(
        body,
        grid=(batch, seq // tm),
        in_specs=[tile] + [whole(a) for a in operands],
        out_specs=tile,
        out_shape=jax.ShapeDtypeStruct(x.shape, x.dtype),
        scratch_shapes=scratch_shapes(tm, seq),
        compiler_params=pltpu.CompilerParams(
            dimension_semantics=("arbitrary", "arbitrary"),
            vmem_limit_bytes=V7X_VMEM_LIMIT_BYTES),
        name=name,
    )(x, *operands)


def _sb_layer(x, norm_gain, w_in, q_gain, k_gain, w_out):
    width = w_out.shape[0]
    n_heads = width // HEAD_DIM
    assert w_in.shape[1] == 4 * width

    def scratch(tm, seq):
        return [
            pltpu.VMEM((tm, x.shape[-1]), _BF16),
            pltpu.VMEM((n_heads, tm, HEAD_DIM), _BF16),
            pltpu.VMEM((n_heads, seq, HEAD_DIM), _BF16),
            pltpu.VMEM((n_heads, seq, HEAD_DIM), _BF16),
            pltpu.VMEM((n_heads, tm, HEAD_DIM), _F32),
            pltpu.VMEM((n_heads, tm, HEAD_DIM), _F32),
            pltpu.VMEM((n_heads, tm, HEAD_DIM), _F32),
        ]

    operands = (norm_gain.reshape(1, -1), w_in.astype(_BF16), q_gain.reshape(1, -1),
                k_gain.reshape(1, -1), w_out.astype(_BF16))
    return _layer_call(functools.partial(_sb_layer_kernel, n_heads=n_heads),
                       x, operands, scratch, "stick_breaking_layer")


def _hg_layer(x, norm_gain, w_in, o_gain, w_out, lb_logits, layer):
    width = w_out.shape[0]
    n_heads = width // HEAD_DIM
    assert w_in.shape[1] == 4 * width

    def scratch(tm, seq):
        per_head = lambda dt: pltpu.VMEM((n_heads, tm, HEAD_DIM), dt)
        return ([pltpu.VMEM((tm, x.shape[-1]), _BF16)] + [per_head(_F32)] * 6
                + [pltpu.VMEM((n_heads, HEAD_DIM, HEAD_DIM), _F32)])

    operands = (norm_gain.reshape(1, -1), w_in.astype(_BF16), o_gain.reshape(1, -1),
                w_out.astype(_BF16), lb_logits)
    return _layer_call(functools.partial(_hg_layer_kernel, n_heads=n_heads, layer=layer),
                       x, operands, scratch, "hgrn2_layer")


def kernel(x, sb_norm, sb_w_in, sb_q_gain, sb_k_gain, sb_w_out,
           hg_norm, hg_w_in, hg_o_gain, hg_w_out, hg_lb_logits):
    depth = hg_lb_logits.shape[0]
    h = x
    for layer in range(depth):
        j = layer // 2
        if layer % 2 == 0:
            h = _sb_layer(h, sb_norm[j], sb_w_in[j], sb_q_gain[j], sb_k_gain[j], sb_w_out[j])
        else:
            h = _hg_layer(h, hg_norm[j], hg_w_in[j], hg_o_gain[j], hg_w_out[j],
                          hg_lb_logits, layer)
    return h
```

```python
import functools

import jax
import jax.numpy as jnp
from jax import lax
from jax.experimental import pallas as pl
from jax.experimental.pallas import tpu as pltpu

HEAD_DIM = 128
RMS_EPS = 1e-6
LOG2_E = 1.4426950408889634
ROW_TILE = 512
NORM_ROWS = 32
SB_BLOCK = 256
SB_SOFTPLUS_CAP = 64.0
SB_DEAD_SKIP_MASS = 160.0
HG_CHUNK = 128
HG_SUBCHUNK = 8
HG_MAX_CHUNK_DECAY = 80.0
SB_PIPELINE_SKEW = 2
HG_PIPELINE_SKEW = 3
V7X_VMEM_LIMIT_BYTES = 56 * 1024 * 1024

_F32 = jnp.float32
_BF16 = jnp.bfloat16
_NT = (((1,), (1,)), ((), ()))
_TN = (((0,), (0,)), ((), ()))


def _rms_rows(x, gain):
    return x * lax.rsqrt(jnp.mean(x * x, axis=-1, keepdims=True) + RMS_EPS) * gain


def _silu(x):
    half = 0.5 * x
    return half + half * jnp.tanh(half)


def _project(x_ref, gain, win_ref, u_scr, width, group_order, store):
    for r in range(0, u_scr.shape[0], NORM_ROWS):
        u_scr[r:r + NORM_ROWS, :] = _rms_rows(x_ref[0, r:r + NORM_ROWS, :], gain).astype(_BF16)
    u = u_scr[...]
    n_heads = width // HEAD_DIM

    def matmul(grp):
        return jnp.dot(u, win_ref[:, grp * width:(grp + 1) * width],
                       preferred_element_type=_F32)

    ahead = matmul(group_order[0])
    for n, grp in enumerate(group_order):
        pg = ahead
        if n + 1 < len(group_order):
            ahead = matmul(group_order[n + 1])
        for h in range(n_heads):
            store(grp, h, pg[:, h * HEAD_DIM:(h + 1) * HEAD_DIM])


def _gated_out(x_ref, out_ref, o_scr, gate_scr, wout_ref, rows):
    o = jnp.concatenate([o_scr[h, rows, :] * gate_scr[h, rows, :]
                         for h in range(o_scr.shape[0])], axis=1)
    out_ref[0, rows, :] = x_ref[0, rows, :] + jnp.dot(
        o.astype(_BF16), wout_ref[...], preferred_element_type=_F32)


def _software_pipeline(items, first, second, third, skew):
    a_out, b_out = {}, {}
    for step in range(len(items) + 2 * skew):
        n1, n2, n3 = step, step - skew, step - 2 * skew
        if n1 < len(items):
            a_out[n1] = first(*items[n1])
        if 0 <= n2 < len(items):
            b_out[n2] = second(*items[n2], *a_out.pop(n2))
        if 0 <= n3 < len(items):
            third(*items[n3], *b_out.pop(n3))


def _sb_layer_kernel(x_ref, ng_ref, win_ref, qg_ref, kg_ref, wout_ref, out_ref,
                     u_scr, q_scr, kt_scr, v_scr, gate_scr, acc_scr, sum_scr, *, n_heads):
    tm = x_ref.shape[1]
    blk = SB_BLOCK
    n_sub = tm // blk
    i = pl.program_id(1)
    row0 = pl.multiple_of(i * tm, tm)
    qg = qg_ref[...] * (HEAD_DIM ** -0.5 * LOG2_E)
    kg = kg_ref[...]

    def store(grp, h, tile):
        if grp == 0:
            q_scr[h] = _rms_rows(tile, qg).astype(_BF16)
        elif grp == 1:
            k = _rms_rows(tile, kg)
            for j in range(n_sub):
                kt_scr[h, n_sub * i + j] = k[j * blk:(j + 1) * blk, :].T.astype(_BF16)
        elif grp == 2:
            v_scr[h, pl.ds(row0, tm), :] = tile.astype(_BF16)
        else:
            gate_scr[h] = _silu(tile)

    _project(x_ref, ng_ref[...], win_ref, u_scr, n_heads * HEAD_DIM, (0, 1, 3, 2), store)

    t_idx = lax.broadcasted_iota(jnp.int32, (blk, blk), 0)
    s_idx = lax.broadcasted_iota(jnp.int32, (blk, blk), 1)
    causal = s_idx < t_idx
    later_keys = jnp.where(causal, 1.0, 0.0).astype(_BF16)


    def gates(h, a, kb, diag):
        q = q_scr[h, a * blk:(a + 1) * blk, :]
        z2 = jnp.dot(q, kt_scr[h, kb], preferred_element_type=_F32)
        skip = jnp.maximum(
            z2, jnp.log(1.0 + jnp.exp2(jnp.minimum(z2, SB_SOFTPLUS_CAP))) * LOG2_E)
        log_beta = z2 - skip
        if diag:
            skip = jnp.where(causal, skip, 0.0)
        row_sum = jnp.sum(skip, axis=-1, keepdims=True)
        return log_beta, skip.astype(_BF16), row_sum

    def weights(h, a, kb, diag, log_beta, skip, row_sum):
        rows = slice(a * blk, (a + 1) * blk)
        skipped = jnp.dot(skip, later_keys, preferred_element_type=_F32)
        if diag:
            sum_scr[h, rows, :] = jnp.broadcast_to(row_sum, (blk, HEAD_DIM))
            return (jnp.where(causal, jnp.exp2(log_beta - skipped), 0.0).astype(_BF16),)
        nearer = sum_scr[h, rows, :]
        sum_scr[h, rows, :] = nearer + row_sum
        return (jnp.exp2(log_beta - skipped
                         - jnp.concatenate([nearer] * (blk // HEAD_DIM), axis=1)).astype(_BF16),)

    def accumulate(h, a, kb, diag, w):
        rows = slice(a * blk, (a + 1) * blk)
        v = v_scr[h, pl.ds(pl.multiple_of(kb * blk, blk), blk), :]
        o = jnp.dot(w, v, preferred_element_type=_F32)
        if diag:
            acc_scr[h, rows, :] = o
        else:
            acc_scr[h, rows, :] += o

    in_tile = [(h, a, n_sub * i + kb, kb == a)
               for a in range(n_sub) for kb in range(a, -1, -1) for h in range(n_heads)]
    _software_pipeline(in_tile, gates, weights, accumulate, SB_PIPELINE_SKEW)


    def least_skip_mass(a):
        rows = slice(a * blk, (a + 1) * blk)
        least = sum_scr[0, rows, :]
        for h in range(1, n_heads):
            least = jnp.minimum(least, sum_scr[h, rows, :])
        return jnp.min(least)

    for a in range(n_sub):
        def kv_cond(state):
            n, least = state
            return jnp.logical_and(n < n_sub * i, least < SB_DEAD_SKIP_MASS)

        def kv_body(state, a=a):
            n, _ = state
            earlier = [(h, a, n_sub * i - 1 - n, False) for h in range(n_heads)]
            _software_pipeline(earlier, gates, weights, accumulate, SB_PIPELINE_SKEW)
            return n + 1, least_skip_mass(a)

        least = least_skip_mass(a) if a > 0 else jnp.float32(0.0)
        lax.while_loop(kv_cond, kv_body, (jnp.int32(0), least))

    _gated_out(x_ref, out_ref, acc_scr, gate_scr, wout_ref, slice(0, tm))


def _hg_layer_kernel(x_ref, ng_ref, win_ref, og_ref, wout_ref, lbl_ref, out_ref,
                     u_scr, q_scr, g_scr, k_scr, i_scr, gate_scr, o_scr, st_scr,
                     *, n_heads, layer):
    tm = x_ref.shape[1]
    n_chunks = tm // HG_CHUNK
    og = og_ref[...]

    @pl.when(pl.program_id(1) == 0)
    def _():
        st_scr[...] = jnp.zeros_like(st_scr)

    lg = lbl_ref[...]
    lmax = jnp.max(lg, axis=0, keepdims=True)
    le = jnp.exp(lg - lmax)
    lb_row = jnp.sum(le[1:layer + 1], axis=0, keepdims=True) / jnp.sum(le, axis=0, keepdims=True)

    chunk_min = [None]

    def store(grp, h, tile):
        if grp == 0:
            q_scr[h] = _silu(tile)
        elif grp == 1:
            lb = lb_row[:, h * HEAD_DIM:(h + 1) * HEAD_DIM]
            half_span = 0.5 * (1.0 - lb)
            f = (lb + half_span) + half_span * jnp.tanh(0.5 * tile)
            f = jnp.maximum(f, lb)
            g = jnp.log(f)
            g_scr[h] = g
            k_scr[h] = 1.0 - f
            for c in range(n_chunks):
                tot = jnp.sum(g[c * HG_CHUNK:(c + 1) * HG_CHUNK], axis=0, keepdims=True)
                chunk_min[0] = tot if chunk_min[0] is None else jnp.minimum(chunk_min[0], tot)
        elif grp == 2:
            i_scr[h] = tile
        else:
            gate_scr[h] = _silu(tile)

    _project(x_ref, ng_ref[...], win_ref, u_scr, n_heads * HEAD_DIM, (1, 0, 3, 2), store)
    matmul_path_ok = jnp.min(chunk_min[0]) >= -HG_MAX_CHUNK_DECAY

    def matmul_path():
        c_t = lax.broadcasted_iota(jnp.int32, (HG_CHUNK, HG_CHUNK), 0)
        c_s = lax.broadcasted_iota(jnp.int32, (HG_CHUNK, HG_CHUNK), 1)
        incl = c_s <= c_t
        earlier_rows = jnp.where(incl, 1.0, 0.0).astype(_BF16)


        def decays(c, h):
            rows = pl.ds(c * HG_CHUNK, HG_CHUNK)
            q = q_scr[h, rows, :]
            k = k_scr[h, rows, :]
            g = g_scr[h, rows, :]
            gc = jnp.dot(earlier_rows, g.astype(_BF16), preferred_element_type=_F32)
            g_last = gc[HG_CHUNK - 1:HG_CHUNK, :]
            g_mid = gc[HG_CHUNK // 2 - 1:HG_CHUNK // 2, :]
            to_mid = jnp.exp(gc - g_mid)
            q_mid = q * to_mid
            k_mid = k / to_mid
            q_in = q_mid * jnp.exp(g_mid)
            k_end = k_mid * jnp.exp(g_last - g_mid)
            return (q_mid.astype(_BF16), k_mid.astype(_BF16), q_in.astype(_BF16),
                    k_end.astype(_BF16), jnp.exp(g_last))

        def intra_scores(c, h, q_mid, k_mid, q_in, k_end, decay_last):
            scores = lax.dot_general(q_mid, k_mid, _NT, preferred_element_type=_F32)
            return jnp.where(incl, scores, 0.0).astype(_BF16), q_in, k_end, decay_last

        def outputs(c, h, scores, q_in, k_end, decay_last):
            rows = pl.ds(c * HG_CHUNK, HG_CHUNK)
            iv = i_scr[h, rows, :].astype(_BF16)
            st = st_scr[h]
            o = jnp.dot(scores, iv, preferred_element_type=_F32)
            o = o + lax.dot_general(q_in, st.astype(_BF16), _NT, preferred_element_type=_F32)
            st_scr[h] = decay_last * st + lax.dot_general(
                iv, k_end, _TN, preferred_element_type=_F32)
            o_scr[h, rows, :] = _rms_rows(o, og)

        items = [(c, h) for c in range(n_chunks) for h in range(n_heads)]
        _software_pipeline(items, decays, intra_scores, outputs, HG_PIPELINE_SKEW)

    def pairwise_path():
        sc = HG_SUBCHUNK
        r_t = lax.broadcasted_iota(jnp.int32, (sc, HEAD_DIM), 0)

        def head_body(h, carry_unused):
            def sub_body(c, st):
                rows = pl.ds(pl.multiple_of(c * sc, sc), sc)
                q = q_scr[h, rows, :]
                k = k_scr[h, rows, :]
                iv = i_scr[h, rows, :]
                g = g_scr[h, rows, :]
                gc = g
                for shift in (1, 2, 4):
                    moved = pltpu.roll(gc, shift, 0)
                    gc = gc + jnp.where(r_t >= shift, moved, 0.0)
                g_last = gc[sc - 1:sc, :]
                o = lax.dot_general((q * jnp.exp(gc)).astype(_BF16), st.astype(_BF16),
                                    _NT, preferred_element_type=_F32)
                for s in range(sc):
                    decay = jnp.exp(jnp.minimum(gc - gc[s:s + 1, :], 0.0))
                    pair = jnp.sum(q * k[s:s + 1, :] * decay, axis=-1, keepdims=True)
                    o = o + jnp.where(r_t >= s, pair, 0.0) * iv[s:s + 1, :]
                k_end = (k * jnp.exp(g_last - gc)).astype(_BF16)
                st = jnp.exp(g_last) * st + lax.dot_general(
                    iv.astype(_BF16), k_end, _TN, preferred_element_type=_F32)
                o_scr[h, rows, :] = _rms_rows(o, og)
                return st

            st_scr[h] = lax.fori_loop(0, tm // sc, sub_body, st_scr[h])
            return carry_unused

        lax.fori_loop(0, n_heads, head_body, 0)

    lax.cond(matmul_path_ok, matmul_path, pairwise_path)
    _gated_out(x_ref, out_ref, o_scr, gate_scr, wout_ref, slice(0, tm))


def _layer_call(body, x, operands, scratch_shapes, name):
    batch, seq, d_model = x.shape
    tm = min(ROW_TILE, seq)
    assert seq % tm == 0 and tm % SB_BLOCK == 0 and tm % HG_CHUNK == 0

    def whole(a):
        return pl.BlockSpec(a.shape, lambda b, t: (0,) * a.ndim)

    tile = pl.BlockSpec((1, tm, d_model), lambda b, t: (b, t, 0))
    return pl.pallas_call(
        body,
        grid=(batch, seq // tm),
        in_specs=[tile] + [whole(a) for a in operands],
        out_specs=tile,
        out_shape=jax.ShapeDtypeStruct(x.shape, x.dtype),
        scratch_shapes=scratch_shapes(tm, seq),
        compiler_params=pltpu.CompilerParams(
            dimension_semantics=("arbitrary", "arbitrary"),
            vmem_limit_bytes=V7X_VMEM_LIMIT_BYTES),
        name=name,
    )(x, *operands)


def _sb_layer(x, norm_gain, w_in, q_gain, k_gain, w_out):
    width = w_out.shape[0]
    n_heads = width // HEAD_DIM
    assert w_in.shape[1] == 4 * width

    def scratch(tm, seq):
        return [
            pltpu.VMEM((tm, x.shape[-1]), _BF16),
            pltpu.VMEM((n_heads, tm, HEAD_DIM), _BF16),
            pltpu.VMEM((n_heads, seq // SB_BLOCK, HEAD_DIM, SB_BLOCK), _BF16),
            pltpu.VMEM((n_heads, seq, HEAD_DIM), _BF16),
            pltpu.VMEM((n_heads, tm, HEAD_DIM), _F32),
            pltpu.VMEM((n_heads, tm, HEAD_DIM), _F32),
            pltpu.VMEM((n_heads, tm, HEAD_DIM), _F32),
        ]

    operands = (norm_gain.reshape(1, -1), w_in.astype(_BF16), q_gain.reshape(1, -1),
                k_gain.reshape(1, -1), w_out.astype(_BF16))
    return _layer_call(functools.partial(_sb_layer_kernel, n_heads=n_heads),
                       x, operands, scratch, "stick_breaking_layer")


def _hg_layer(x, norm_gain, w_in, o_gain, w_out, lb_logits, layer):
    width = w_out.shape[0]
    n_heads = width // HEAD_DIM
    assert w_in.shape[1] == 4 * width

    def scratch(tm, seq):
        per_head = lambda dt: pltpu.VMEM((n_heads, tm, HEAD_DIM), dt)
        return ([pltpu.VMEM((tm, x.shape[-1]), _BF16)] + [per_head(_F32)] * 6
                + [pltpu.VMEM((n_heads, HEAD_DIM, HEAD_DIM), _F32)])

    operands = (norm_gain.reshape(1, -1), w_in.astype(_BF16), o_gain.reshape(1, -1),
                w_out.astype(_BF16), lb_logits)
    return _layer_call(functools.partial(_hg_layer_kernel, n_heads=n_heads, layer=layer),
                       x, operands, scratch, "hgrn2_layer")


def kernel(x, sb_norm, sb_w_in, sb_q_gain, sb_k_gain, sb_w_out,
           hg_norm, hg_w_in, hg_o_gain, hg_w_out, hg_lb_logits):
    depth = hg_lb_logits.shape[0]
    h = x
    for layer in range(depth):
        j = layer // 2
        if layer % 2 == 0:
            h = _sb_layer(h, sb_norm[j], sb_w_in[j], sb_q_gain[j], sb_k_gain[j], sb_w_out[j])
        else:
            h = _hg_layer(h, hg_norm[j], hg_w_in[j], hg_o_gain[j], hg_w_out[j],
                          hg_lb_logits, layer)
    return h
```

```python
import functools

import jax
import jax.numpy as jnp
from jax import lax
from jax.experimental import pallas as pl
from jax.experimental.pallas import tpu as pltpu

HEAD_DIM = 128
RMS_EPS = 1e-6
LOG2_E = 1.4426950408889634
ROW_TILE = 512
NORM_ROWS = 32
SB_BLOCK = 256
SB_SOFTPLUS_CAP = 64.0
SB_DEAD_SKIP_MASS = 160.0
HG_CHUNK = 128
HG_SUBCHUNK = 8
HG_MAX_CHUNK_DECAY = 80.0
SB_PIPELINE_SKEW = 2
HG_PIPELINE_SKEW = 3
MARGIN_ROWS = 8
NO_CONSTRAINT = 1e30
V7X_VMEM_LIMIT_BYTES = 56 * 1024 * 1024

_F32 = jnp.float32
_BF16 = jnp.bfloat16
_NT = (((1,), (1,)), ((), ()))
_TN = (((0,), (0,)), ((), ()))


def _rms_rows(x, gain):
    return x * lax.rsqrt(jnp.mean(x * x, axis=-1, keepdims=True) + RMS_EPS) * gain


def _silu(x):
    half = 0.5 * x
    return half + half * jnp.tanh(half)


def _normalise(x_ref, gain, u_scr):
    for r in range(0, u_scr.shape[0], NORM_ROWS):
        u_scr[r:r + NORM_ROWS, :] = _rms_rows(x_ref[0, r:r + NORM_ROWS, :], gain).astype(_BF16)


def _project(u_scr, win_ref, width, group_order, store):
    u = u_scr[...]
    n_heads = width // HEAD_DIM

    def matmul(grp):
        return jnp.dot(u, win_ref[:, grp * width:(grp + 1) * width],
                       preferred_element_type=_F32)

    ahead = matmul(group_order[0])
    for n, grp in enumerate(group_order):
        pg = ahead
        if n + 1 < len(group_order):
            ahead = matmul(group_order[n + 1])
        for h in range(n_heads):
            store(grp, h, pg[:, h * HEAD_DIM:(h + 1) * HEAD_DIM])


def _gated_out(x_ref, out_ref, o_scr, gate_scr, wout_ref):
    o = jnp.concatenate([o_scr[h] * gate_scr[h] for h in range(o_scr.shape[0])], axis=1)
    out_ref[0] = x_ref[0] + jnp.dot(o.astype(_BF16), wout_ref[...], preferred_element_type=_F32)


def _least_rows(x):
    least = x[:MARGIN_ROWS]
    for r in range(MARGIN_ROWS, x.shape[0], MARGIN_ROWS):
        least = jnp.minimum(least, x[r:r + MARGIN_ROWS])
    return least


def _software_pipeline(items, first, second, third, skew):
    a_out, b_out = {}, {}
    for step in range(len(items) + 2 * skew):
        n1, n2, n3 = step, step - skew, step - 2 * skew
        if n1 < len(items):
            a_out[n1] = first(*items[n1])
        if 0 <= n2 < len(items):
            b_out[n2] = second(*items[n2], *a_out.pop(n2))
        if 0 <= n3 < len(items):
            third(*items[n3], *b_out.pop(n3))


def _sb_layer_kernel(x_ref, ng_ref, win_ref, qg_ref, kg_ref, wout_ref, *rest,
                     n_heads, windowed):
    if windowed:
        out_ref, margin_ref = rest[:2]
    else:
        out_ref, margin_ref = rest[0], None
    u_scr, q_scr, k_scr, v_scr, gate_scr, acc_scr, sum_scr = rest[-7:]
    tm = x_ref.shape[1]
    blk = SB_BLOCK
    n_sub = tm // blk
    i = pl.program_id(1)
    row0 = pl.multiple_of(i * tm, tm)
    qg = qg_ref[...] * (HEAD_DIM ** -0.5 * LOG2_E)
    kg = kg_ref[...]

    def store(grp, h, tile):
        if grp == 0:
            q_scr[h] = _rms_rows(tile, qg).astype(_BF16)
        elif grp == 1:
            k_scr[h, pl.ds(row0, tm), :] = _rms_rows(tile, kg).astype(_BF16)
        elif grp == 2:
            v_scr[h, pl.ds(row0, tm), :] = tile.astype(_BF16)
        else:
            gate_scr[h] = _silu(tile)

    _normalise(x_ref, ng_ref[...], u_scr)
    _project(u_scr, win_ref, n_heads * HEAD_DIM, (0, 1, 3, 2), store)

    t_idx = lax.broadcasted_iota(jnp.int32, (blk, blk), 0)
    s_idx = lax.broadcasted_iota(jnp.int32, (blk, blk), 1)
    causal = s_idx < t_idx
    later_keys = jnp.where(causal, 1.0, 0.0).astype(_BF16)
    has_earlier_tiles = i > 0


    def gates(h, a, r0, kind):
        q = q_scr[h, a * blk:(a + 1) * blk, :]
        z2 = lax.dot_general(q, k_scr[h, pl.ds(r0, blk), :], _NT, preferred_element_type=_F32)
        skip = jnp.maximum(
            z2, jnp.log(1.0 + jnp.exp2(jnp.minimum(z2, SB_SOFTPLUS_CAP))) * LOG2_E)
        log_beta = z2 - skip
        if kind == "diag":
            skip = jnp.where(causal, skip, 0.0)
        row_sum = jnp.sum(skip, axis=-1, keepdims=True)
        return log_beta, skip.astype(_BF16), row_sum

    def weights(h, a, r0, kind, log_beta, skip, row_sum):
        rows = slice(a * blk, (a + 1) * blk)
        skipped = jnp.dot(skip, later_keys, preferred_element_type=_F32)
        if kind == "diag":
            sum_scr[h, rows, :] = jnp.broadcast_to(row_sum, (blk, HEAD_DIM))
            return (jnp.where(causal, jnp.exp2(log_beta - skipped), 0.0).astype(_BF16),)
        nearer = sum_scr[h, rows, :]
        sum_scr[h, rows, :] = nearer + row_sum
        return (jnp.exp2(log_beta - skipped
                         - jnp.concatenate([nearer] * (blk // HEAD_DIM), axis=1)).astype(_BF16),)

    def accumulate(h, a, r0, kind, w):
        rows = slice(a * blk, (a + 1) * blk)
        o = jnp.dot(w, v_scr[h, pl.ds(r0, blk), :], preferred_element_type=_F32)
        if kind == "diag":
            acc_scr[h, rows, :] = o
        else:
            acc_scr[h, rows, :] += o

    def least_skip_mass(a):
        rows = slice(a * blk, (a + 1) * blk)
        least = sum_scr[0, rows, :]
        for h in range(1, n_heads):
            least = jnp.minimum(least, sum_scr[h, rows, :])
        return least

    in_tile = [(h, a, row0 + kb * blk, "diag" if kb == a else "full")
               for a in range(n_sub) for kb in range(a, -1, -1) for h in range(n_heads)]


    if windowed:
        _software_pipeline(in_tile, gates, weights, accumulate, SB_PIPELINE_SKEW)

        @pl.when(has_earlier_tiles)
        def _():
            before = [(h, 0, pl.multiple_of(row0 - blk, blk), "full") for h in range(n_heads)]
            _software_pipeline(before, gates, weights, accumulate, SB_PIPELINE_SKEW)

        least = least_skip_mass(0)
        for a in range(1, n_sub):
            least = jnp.minimum(least, least_skip_mass(a))
        margin_ref[0] = jnp.where(has_earlier_tiles, _least_rows(least), NO_CONSTRAINT)
    else:
        _software_pipeline(in_tile, gates, weights, accumulate, SB_PIPELINE_SKEW)
        for a in range(n_sub):
            def kv_cond(state):
                n, least = state
                return jnp.logical_and(n < n_sub * i, least < SB_DEAD_SKIP_MASS)

            def kv_body(state, a=a):
                n, _ = state
                r0 = pl.multiple_of(row0 - (n + 1) * blk, blk)
                earlier = [(h, a, r0, "full") for h in range(n_heads)]
                _software_pipeline(earlier, gates, weights, accumulate, SB_PIPELINE_SKEW)
                return n + 1, jnp.min(least_skip_mass(a))

            least = jnp.min(least_skip_mass(a)) if a > 0 else jnp.float32(0.0)
            lax.while_loop(kv_cond, kv_body, (jnp.int32(0), least))

    _gated_out(x_ref, out_ref, acc_scr, gate_scr, wout_ref)


def _hg_layer_kernel(x_ref, ng_ref, win_ref, og_ref, wout_ref, lbl_ref, *rest,
                     n_heads, layer, guarded):
    if guarded:
        out_ref, margin_ref = rest[0], None
    else:
        out_ref, margin_ref = rest[:2]
    u_scr, q_scr, g_scr, k_scr, i_scr, gate_scr, o_scr, st_scr = rest[-8:]
    tm = x_ref.shape[1]
    n_chunks = tm // HG_CHUNK
    og = og_ref[...]

    @pl.when(pl.program_id(1) == 0)
    def _():
        st_scr[...] = jnp.zeros_like(st_scr)

    lg = lbl_ref[...]
    lmax = jnp.max(lg, axis=0, keepdims=True)
    le = jnp.exp(lg - lmax)
    lb_row = jnp.sum(le[1:layer + 1], axis=0, keepdims=True) / jnp.sum(le, axis=0, keepdims=True)

    chunk_min = [None]

    def store(grp, h, tile):
        if grp == 0:
            q_scr[h] = _silu(tile)
        elif grp == 1:
            lb = lb_row[:, h * HEAD_DIM:(h + 1) * HEAD_DIM]
            half_span = 0.5 * (1.0 - lb)
            f = (lb + half_span) + half_span * jnp.tanh(0.5 * tile)
            f = jnp.maximum(f, lb)
            g = jnp.log(f)
            g_scr[h] = g
            k_scr[h] = 1.0 - f
            for c in range(n_chunks):
                tot = jnp.sum(g[c * HG_CHUNK:(c + 1) * HG_CHUNK], axis=0, keepdims=True)
                chunk_min[0] = tot if chunk_min[0] is None else jnp.minimum(chunk_min[0], tot)
        elif grp == 2:
            i_scr[h] = tile
        else:
            gate_scr[h] = _silu(tile)

    _normalise(x_ref, ng_ref[...], u_scr)
    _project(u_scr, win_ref, n_heads * HEAD_DIM, (1, 0, 3, 2), store)
    least_chunk_decay = chunk_min[0]

    def matmul_path():
        c_t = lax.broadcasted_iota(jnp.int32, (HG_CHUNK, HG_CHUNK), 0)
        c_s = lax.broadcasted_iota(jnp.int32, (HG_CHUNK, HG_CHUNK), 1)
        incl = c_s <= c_t
        earlier_rows = jnp.where(incl, 1.0, 0.0).astype(_BF16)


        def decays(c, h):
            rows = pl.ds(c * HG_CHUNK, HG_CHUNK)
            q = q_scr[h, rows, :]
            k = k_scr[h, rows, :]
            g = g_scr[h, rows, :]
            gc = jnp.dot(earlier_rows, g.astype(_BF16), preferred_element_type=_F32)
            g_last = gc[HG_CHUNK - 1:HG_CHUNK, :]
            g_mid = gc[HG_CHUNK // 2 - 1:HG_CHUNK // 2, :]
            to_mid = jnp.exp(gc - g_mid)
            q_mid = q * to_mid
            k_mid = k / to_mid
            q_in = q_mid * jnp.exp(g_mid)
            k_end = k_mid * jnp.exp(g_last - g_mid)
            return (q_mid.astype(_BF16), k_mid.astype(_BF16), q_in.astype(_BF16),
                    k_end.astype(_BF16), jnp.exp(g_last))

        def intra_scores(c, h, q_mid, k_mid, q_in, k_end, decay_last):
            scores = lax.dot_general(q_mid, k_mid, _NT, preferred_element_type=_F32)
            return jnp.where(incl, scores, 0.0).astype(_BF16), q_in, k_end, decay_last

        def outputs(c, h, scores, q_in, k_end, decay_last):
            rows = pl.ds(c * HG_CHUNK, HG_CHUNK)
            iv = i_scr[h, rows, :].astype(_BF16)
            st = st_scr[h]
            o = jnp.dot(scores, iv, preferred_element_type=_F32)
            o = o + lax.dot_general(q_in, st.astype(_BF16), _NT, preferred_element_type=_F32)
            st_scr[h] = decay_last * st + lax.dot_general(
                iv, k_end, _TN, preferred_element_type=_F32)
            o_scr[h, rows, :] = _rms_rows(o, og)

        items = [(c, h) for c in range(n_chunks) for h in range(n_heads)]
        _software_pipeline(items, decays, intra_scores, outputs, HG_PIPELINE_SKEW)

    def pairwise_path():
        sc = HG_SUBCHUNK
        r_t = lax.broadcasted_iota(jnp.int32, (sc, HEAD_DIM), 0)

        def head_body(h, carry_unused):
            def sub_body(c, st):
                rows = pl.ds(pl.multiple_of(c * sc, sc), sc)
                q = q_scr[h, rows, :]
                k = k_scr[h, rows, :]
                iv = i_scr[h, rows, :]
                g = g_scr[h, rows, :]
                gc = g
                for shift in (1, 2, 4):
                    moved = pltpu.roll(gc, shift, 0)
                    gc = gc + jnp.where(r_t >= shift, moved, 0.0)
                g_last = gc[sc - 1:sc, :]
                o = lax.dot_general((q * jnp.exp(gc)).astype(_BF16), st.astype(_BF16),
                                    _NT, preferred_element_type=_F32)
                for s in range(sc):
                    decay = jnp.exp(jnp.minimum(gc - gc[s:s + 1, :], 0.0))
                    pair = jnp.sum(q * k[s:s + 1, :] * decay, axis=-1, keepdims=True)
                    o = o + jnp.where(r_t >= s, pair, 0.0) * iv[s:s + 1, :]
                k_end = (k * jnp.exp(g_last - gc)).astype(_BF16)
                st = jnp.exp(g_last) * st + lax.dot_general(
                    iv.astype(_BF16), k_end, _TN, preferred_element_type=_F32)
                o_scr[h, rows, :] = _rms_rows(o, og)
                return st

            st_scr[h] = lax.fori_loop(0, tm // sc, sub_body, st_scr[h])
            return carry_unused

        lax.fori_loop(0, n_heads, head_body, 0)

    if guarded:
        lax.cond(jnp.min(least_chunk_decay) >= -HG_MAX_CHUNK_DECAY, matmul_path, pairwise_path)
    else:
        margin_ref[0] = jnp.broadcast_to(least_chunk_decay, (MARGIN_ROWS, HEAD_DIM))
        matmul_path()
    _gated_out(x_ref, out_ref, o_scr, gate_scr, wout_ref)


def _layer_call(body, x, operands, scratch_shapes, name, with_margin):
    batch, seq, d_model = x.shape
    tm = min(ROW_TILE, seq)
    assert seq % tm == 0 and tm % SB_BLOCK == 0 and tm % HG_CHUNK == 0
    steps = seq // tm

    def whole(a):
        return pl.BlockSpec(a.shape, lambda b, t: (0,) * a.ndim)

    tile = pl.BlockSpec((1, tm, d_model), lambda b, t: (b, t, 0))
    out_specs, out_shape = tile, jax.ShapeDtypeStruct(x.shape, x.dtype)
    if with_margin:
        out_specs = (tile, pl.BlockSpec((1, MARGIN_ROWS, HEAD_DIM),
                                        lambda b, t: (b * steps + t, 0, 0)))
        out_shape = (out_shape,
                     jax.ShapeDtypeStruct((batch * steps, MARGIN_ROWS, HEAD_DIM), _F32))
    return pl.pallas_call(
        body,
        grid=(batch, steps),
        in_specs=[tile] + [whole(a) for a in operands],
        out_specs=out_specs,
        out_shape=out_shape,
        scratch_shapes=scratch_shapes(tm, seq),
        compiler_params=pltpu.CompilerParams(
            dimension_semantics=("arbitrary", "arbitrary"),
            vmem_limit_bytes=V7X_VMEM_LIMIT_BYTES),
        name=name,
    )(x, *operands)


def _sb_layer(x, norm_gain, w_in, q_gain, k_gain, w_out):
    width = w_out.shape[0]
    n_heads = width // HEAD_DIM
    assert w_in.shape[1] == 4 * width

    def scratch(tm, seq):
        return [
            pltpu.VMEM((tm, x.shape[-1]), _BF16),
            pltpu.VMEM((n_heads, tm, HEAD_DIM), _BF16),
            pltpu.VMEM((n_heads, seq, HEAD_DIM), _BF16),
            pltpu.VMEM((n_heads, seq, HEAD_DIM), _BF16),
            pltpu.VMEM((n_heads, tm, HEAD_DIM), _F32),
            pltpu.VMEM((n_heads, tm, HEAD_DIM), _F32),
            pltpu.VMEM((n_heads, tm, HEAD_DIM), _F32),
        ]

    operands = (norm_gain.reshape(1, -1), w_in.astype(_BF16), q_gain.reshape(1, -1),
                k_gain.reshape(1, -1), w_out.astype(_BF16))

    def run(windowed):
        return _layer_call(
            functools.partial(_sb_layer_kernel, n_heads=n_heads, windowed=windowed),
            x, operands, scratch,
            "stick_breaking_window" if windowed else "stick_breaking_layer", windowed)

    out, margin = run(True)
    return lax.cond(jnp.min(margin) >= SB_DEAD_SKIP_MASS, lambda: out, lambda: run(False))


def _hg_layer(x, norm_gain, w_in, o_gain, w_out, lb_logits, layer):
    width = w_out.shape[0]
    n_heads = width // HEAD_DIM
    assert w_in.shape[1] == 4 * width

    def scratch(tm, seq):
        per_head = lambda dt: pltpu.VMEM((n_heads, tm, HEAD_DIM), dt)
        return ([pltpu.VMEM((tm, x.shape[-1]), _BF16)] + [per_head(_F32)] * 6
                + [pltpu.VMEM((n_heads, HEAD_DIM, HEAD_DIM), _F32)])

    operands = (norm_gain.reshape(1, -1), w_in.astype(_BF16), o_gain.reshape(1, -1),
                w_out.astype(_BF16), lb_logits)

    def run(guarded):
        return _layer_call(
            functools.partial(_hg_layer_kernel, n_heads=n_heads, layer=layer, guarded=guarded),
            x, operands, scratch, "hgrn2_layer" if guarded else "hgrn2_matmul_path",
            not guarded)

    out, margin = run(False)
    return lax.cond(jnp.min(margin) >= -HG_MAX_CHUNK_DECAY, lambda: out, lambda: run(True))


def kernel(x, sb_norm, sb_w_in, sb_q_gain, sb_k_gain, sb_w_out,
           hg_norm, hg_w_in, hg_o_gain, hg_w_out, hg_lb_logits):
    depth = hg_lb_logits.shape[0]
    h = x
    for layer in range(depth):
        j = layer // 2
        if layer % 2 == 0:
            h = _sb_layer(h, sb_norm[j], sb_w_in[j], sb_q_gain[j], sb_k_gain[j], sb_w_out[j])
        else:
            h = _hg_layer(h, hg_norm[j], hg_w_in[j], hg_o_gain[j], hg_w_out[j],
                          hg_lb_logits, layer)
    return h
```

```python
import functools

import jax
import jax.numpy as jnp
from jax import lax
from jax.experimental import pallas as pl
from jax.experimental.pallas import tpu as pltpu

HEAD_DIM = 128
RMS_EPS = 1e-6
LOG2_E = 1.4426950408889634
ROW_TILE = 512
NORM_ROWS = 32
SB_BLOCK = 256
SB_SOFTPLUS_CAP = 64.0
SB_DEAD_SKIP_MASS = 160.0
HG_CHUNK = 128
HG_SUBCHUNK = 8
HG_MAX_CHUNK_DECAY = 80.0
SB_PIPELINE_SKEW = 3
HG_PIPELINE_SKEW = 4
MARGIN_ROWS = 8
NO_CONSTRAINT = 1e30
V7X_VMEM_LIMIT_BYTES = 56 * 1024 * 1024

_F32 = jnp.float32
_BF16 = jnp.bfloat16
_NT = (((1,), (1,)), ((), ()))
_TN = (((0,), (0,)), ((), ()))


def _rms_rows(x, gain):
    return x * lax.rsqrt(jnp.mean(x * x, axis=-1, keepdims=True) + RMS_EPS) * gain


def _silu(x):
    half = 0.5 * x
    return half + half * jnp.tanh(half)


def _normalise(x_ref, gain, u_scr):
    for r in range(0, u_scr.shape[0], NORM_ROWS):
        u_scr[r:r + NORM_ROWS, :] = _rms_rows(x_ref[0, r:r + NORM_ROWS, :], gain).astype(_BF16)


def _project(u_scr, win_ref, width, group_order, store):
    u = u_scr[...]
    n_heads = width // HEAD_DIM

    def matmul(grp):
        return jnp.dot(u, win_ref[:, grp * width:(grp + 1) * width],
                       preferred_element_type=_F32)

    ahead = matmul(group_order[0])
    for n, grp in enumerate(group_order):
        pg = ahead
        if n + 1 < len(group_order):
            ahead = matmul(group_order[n + 1])
        for h in range(n_heads):
            store(grp, h, pg[:, h * HEAD_DIM:(h + 1) * HEAD_DIM])


def _gated_out(x_ref, out_ref, o_scr, gate_scr, wout_ref):
    o = jnp.concatenate([o_scr[h] * gate_scr[h] for h in range(o_scr.shape[0])], axis=1)
    out_ref[0] = x_ref[0] + jnp.dot(o.astype(_BF16), wout_ref[...], preferred_element_type=_F32)


def _least_rows(x):
    least = x[:MARGIN_ROWS]
    for r in range(MARGIN_ROWS, x.shape[0], MARGIN_ROWS):
        least = jnp.minimum(least, x[r:r + MARGIN_ROWS])
    return least


def _software_pipeline(items, first, second, third, skew):
    a_out, b_out = {}, {}
    for step in range(len(items) + 2 * skew):
        n1, n2, n3 = step, step - skew, step - 2 * skew
        if n1 < len(items):
            a_out[n1] = first(*items[n1])
        if 0 <= n2 < len(items):
            b_out[n2] = second(*items[n2], *a_out.pop(n2))
        if 0 <= n3 < len(items):
            third(*items[n3], *b_out.pop(n3))


def _sb_layer_kernel(x_ref, ng_ref, win_ref, qg_ref, kg_ref, wout_ref, *rest,
                     n_heads, windowed):
    if windowed:
        out_ref, margin_ref = rest[:2]
    else:
        out_ref, margin_ref = rest[0], None
    u_scr, q_scr, k_scr, v_scr, gate_scr, acc_scr, sum_scr = rest[-7:]
    tm = x_ref.shape[1]
    blk = SB_BLOCK
    n_sub = tm // blk
    i = pl.program_id(1)
    row0 = pl.multiple_of(i * tm, tm)
    qg = qg_ref[...] * (HEAD_DIM ** -0.5 * LOG2_E)
    kg = kg_ref[...]

    def store(grp, h, tile):
        if grp == 0:
            q_scr[h] = _rms_rows(tile, qg).astype(_BF16)
        elif grp == 1:
            k_scr[h, pl.ds(row0, tm), :] = _rms_rows(tile, kg).astype(_BF16)
        elif grp == 2:
            v_scr[h, pl.ds(row0, tm), :] = tile.astype(_BF16)
        else:
            gate_scr[h] = _silu(tile)

    _normalise(x_ref, ng_ref[...], u_scr)
    _project(u_scr, win_ref, n_heads * HEAD_DIM, (0, 1, 3, 2), store)

    t_idx = lax.broadcasted_iota(jnp.int32, (blk, blk), 0)
    s_idx = lax.broadcasted_iota(jnp.int32, (blk, blk), 1)
    causal = s_idx < t_idx
    later_keys = jnp.where(causal, 1.0, 0.0).astype(_BF16)
    has_earlier_tiles = i > 0


    def gates(h, a, r0, kind):
        q = q_scr[h, a * blk:(a + 1) * blk, :]
        z2 = lax.dot_general(q, k_scr[h, pl.ds(r0, blk), :], _NT, preferred_element_type=_F32)
        skip = jnp.maximum(
            z2, jnp.log(1.0 + jnp.exp2(jnp.minimum(z2, SB_SOFTPLUS_CAP))) * LOG2_E)
        log_beta = z2 - skip
        if kind == "diag":
            skip = jnp.where(causal, skip, 0.0)
        row_sum = jnp.sum(skip, axis=-1, keepdims=True)
        return log_beta, skip.astype(_BF16), row_sum

    def weights(h, a, r0, kind, log_beta, skip, row_sum):
        rows = slice(a * blk, (a + 1) * blk)
        skipped = jnp.dot(skip, later_keys, preferred_element_type=_F32)
        if kind == "diag":
            sum_scr[h, rows, :] = jnp.broadcast_to(row_sum, (blk, HEAD_DIM))
            return (jnp.where(causal, jnp.exp2(log_beta - skipped), 0.0).astype(_BF16),)
        nearer = sum_scr[h, rows, :]
        sum_scr[h, rows, :] = nearer + row_sum
        return (jnp.exp2(log_beta - skipped
                         - jnp.concatenate([nearer] * (blk // HEAD_DIM), axis=1)).astype(_BF16),)

    def accumulate(h, a, r0, kind, w):
        rows = slice(a * blk, (a + 1) * blk)
        o = jnp.dot(w, v_scr[h, pl.ds(r0, blk), :], preferred_element_type=_F32)
        if kind == "diag":
            acc_scr[h, rows, :] = o
        else:
            acc_scr[h, rows, :] += o

    def least_skip_mass(a):
        rows = slice(a * blk, (a + 1) * blk)
        least = sum_scr[0, rows, :]
        for h in range(1, n_heads):
            least = jnp.minimum(least, sum_scr[h, rows, :])
        return least

    in_tile = [(h, a, row0 + kb * blk, "diag" if kb == a else "full")
               for a in range(n_sub) for kb in range(a, -1, -1) for h in range(n_heads)]


    if windowed:
        def window_with_before():
            before = [(h, 0, pl.multiple_of(row0 - blk, blk), "full") for h in range(n_heads)]
            window = in_tile[:n_heads] + before + in_tile[n_heads:]
            _software_pipeline(window, gates, weights, accumulate, SB_PIPELINE_SKEW)

        def window_of_first_tile():
            _software_pipeline(in_tile, gates, weights, accumulate, SB_PIPELINE_SKEW)

        lax.cond(has_earlier_tiles, window_with_before, window_of_first_tile)
        least = least_skip_mass(0)
        for a in range(1, n_sub):
            least = jnp.minimum(least, least_skip_mass(a))
        margin_ref[0] = jnp.where(has_earlier_tiles, _least_rows(least), NO_CONSTRAINT)
    else:
        _software_pipeline(in_tile, gates, weights, accumulate, SB_PIPELINE_SKEW)
        for a in range(n_sub):
            def kv_cond(state):
                n, least = state
                return jnp.logical_and(n < n_sub * i, least < SB_DEAD_SKIP_MASS)

            def kv_body(state, a=a):
                n, _ = state
                r0 = pl.multiple_of(row0 - (n + 1) * blk, blk)
                earlier = [(h, a, r0, "full") for h in range(n_heads)]
                _software_pipeline(earlier, gates, weights, accumulate, SB_PIPELINE_SKEW)
                return n + 1, jnp.min(least_skip_mass(a))

            least = jnp.min(least_skip_mass(a)) if a > 0 else jnp.float32(0.0)
            lax.while_loop(kv_cond, kv_body, (jnp.int32(0), least))

    _gated_out(x_ref, out_ref, acc_scr, gate_scr, wout_ref)


def _hg_layer_kernel(x_ref, ng_ref, win_ref, og_ref, wout_ref, lbl_ref, *rest,
                     n_heads, layer, guarded):
    if guarded:
        out_ref, margin_ref = rest[0], None
    else:
        out_ref, margin_ref = rest[:2]
    u_scr, q_scr, g_scr, k_scr, i_scr, gate_scr, o_scr, st_scr = rest[-8:]
    tm = x_ref.shape[1]
    n_chunks = tm // HG_CHUNK
    og = og_ref[...]

    @pl.when(pl.program_id(1) == 0)
    def _():
        st_scr[...] = jnp.zeros_like(st_scr)

    lg = lbl_ref[...]
    lmax = jnp.max(lg, axis=0, keepdims=True)
    le = jnp.exp(lg - lmax)
    lb_row = jnp.sum(le[1:layer + 1], axis=0, keepdims=True) / jnp.sum(le, axis=0, keepdims=True)

    chunk_min = [None]

    def store(grp, h, tile):
        if grp == 0:
            q_scr[h] = _silu(tile)
        elif grp == 1:
            lb = lb_row[:, h * HEAD_DIM:(h + 1) * HEAD_DIM]
            half_span = 0.5 * (1.0 - lb)
            f = (lb + half_span) + half_span * jnp.tanh(0.5 * tile)
            f = jnp.maximum(f, lb)
            g = jnp.log(f)
            g_scr[h] = g
            k_scr[h] = 1.0 - f
            for c in range(n_chunks):
                tot = jnp.sum(g[c * HG_CHUNK:(c + 1) * HG_CHUNK], axis=0, keepdims=True)
                chunk_min[0] = tot if chunk_min[0] is None else jnp.minimum(chunk_min[0], tot)
        elif grp == 2:
            i_scr[h] = tile
        else:
            gate_scr[h] = _silu(tile)

    _normalise(x_ref, ng_ref[...], u_scr)
    _project(u_scr, win_ref, n_heads * HEAD_DIM, (1, 0, 3, 2), store)
    least_chunk_decay = chunk_min[0]

    def matmul_path():
        c_t = lax.broadcasted_iota(jnp.int32, (HG_CHUNK, HG_CHUNK), 0)
        c_s = lax.broadcasted_iota(jnp.int32, (HG_CHUNK, HG_CHUNK), 1)
        incl = c_s <= c_t
        earlier_rows = jnp.where(incl, 1.0, 0.0).astype(_BF16)


        def decays(c, h):
            rows = pl.ds(c * HG_CHUNK, HG_CHUNK)
            q = q_scr[h, rows, :]
            k = k_scr[h, rows, :]
            g = g_scr[h, rows, :]
            gc = jnp.dot(earlier_rows, g.astype(_BF16), preferred_element_type=_F32)
            g_last = gc[HG_CHUNK - 1:HG_CHUNK, :]
            g_mid = gc[HG_CHUNK // 2 - 1:HG_CHUNK // 2, :]
            to_mid = jnp.exp(gc - g_mid)
            q_mid = q * to_mid
            k_mid = k / to_mid
            q_in = q_mid * jnp.exp(g_mid)
            k_end = k_mid * jnp.exp(g_last - g_mid)
            return (q_mid.astype(_BF16), k_mid.astype(_BF16), q_in.astype(_BF16),
                    k_end.astype(_BF16), jnp.exp(g_last))

        def intra_scores(c, h, q_mid, k_mid, q_in, k_end, decay_last):
            scores = lax.dot_general(q_mid, k_mid, _NT, preferred_element_type=_F32)
            return jnp.where(incl, scores, 0.0).astype(_BF16), q_in, k_end, decay_last

        def outputs(c, h, scores, q_in, k_end, decay_last):
            rows = pl.ds(c * HG_CHUNK, HG_CHUNK)
            iv = i_scr[h, rows, :].astype(_BF16)
            st = st_scr[h]
            o = jnp.dot(scores, iv, preferred_element_type=_F32)
            o = o + lax.dot_general(q_in, st.astype(_BF16), _NT, preferred_element_type=_F32)
            st_scr[h] = decay_last * st + lax.dot_general(
                iv, k_end, _TN, preferred_element_type=_F32)
            o_scr[h, rows, :] = _rms_rows(o, og)

        items = [(c, h) for c in range(n_chunks) for h in range(n_heads)]
        _software_pipeline(items, decays, intra_scores, outputs, HG_PIPELINE_SKEW)

    def pairwise_path():
        sc = HG_SUBCHUNK
        r_t = lax.broadcasted_iota(jnp.int32, (sc, HEAD_DIM), 0)

        def head_body(h, carry_unused):
            def sub_body(c, st):
                rows = pl.ds(pl.multiple_of(c * sc, sc), sc)
                q = q_scr[h, rows, :]
                k = k_scr[h, rows, :]
                iv = i_scr[h, rows, :]
                g = g_scr[h, rows, :]
                gc = g
                for shift in (1, 2, 4):
                    moved = pltpu.roll(gc, shift, 0)
                    gc = gc + jnp.where(r_t >= shift, moved, 0.0)
                g_last = gc[sc - 1:sc, :]
                o = lax.dot_general((q * jnp.exp(gc)).astype(_BF16), st.astype(_BF16),
                                    _NT, preferred_element_type=_F32)
                for s in range(sc):
                    decay = jnp.exp(jnp.minimum(gc - gc[s:s + 1, :], 0.0))
                    pair = jnp.sum(q * k[s:s + 1, :] * decay, axis=-1, keepdims=True)
                    o = o + jnp.where(r_t >= s, pair, 0.0) * iv[s:s + 1, :]
                k_end = (k * jnp.exp(g_last - gc)).astype(_BF16)
                st = jnp.exp(g_last) * st + lax.dot_general(
                    iv.astype(_BF16), k_end, _TN, preferred_element_type=_F32)
                o_scr[h, rows, :] = _rms_rows(o, og)
                return st

            st_scr[h] = lax.fori_loop(0, tm // sc, sub_body, st_scr[h])
            return carry_unused

        lax.fori_loop(0, n_heads, head_body, 0)

    if guarded:
        lax.cond(jnp.min(least_chunk_decay) >= -HG_MAX_CHUNK_DECAY, matmul_path, pairwise_path)
    else:
        margin_ref[0] = jnp.broadcast_to(least_chunk_decay, (MARGIN_ROWS, HEAD_DIM))
        matmul_path()
    _gated_out(x_ref, out_ref, o_scr, gate_scr, wout_ref)


def _layer_call(body, x, operands, scratch_shapes, name, with_margin):
    batch, seq, d_model = x.shape
    tm = min(ROW_TILE, seq)
    assert seq % tm == 0 and tm % SB_BLOCK == 0 and tm % HG_CHUNK == 0
    steps = seq // tm

    def whole(a):
        return pl.BlockSpec(a.shape, lambda b, t: (0,) * a.ndim)

    tile = pl.BlockSpec((1, tm, d_model), lambda b, t: (b, t, 0))
    out_specs, out_shape = tile, jax.ShapeDtypeStruct(x.shape, x.dtype)
    if with_margin:
        out_specs = (tile, pl.BlockSpec((1, MARGIN_ROWS, HEAD_DIM),
                                        lambda b, t: (b * steps + t, 0, 0)))
        out_shape = (out_shape,
                     jax.ShapeDtypeStruct((batch * steps, MARGIN_ROWS, HEAD_DIM), _F32))
    return pl.pallas_call(
        body,
        grid=(batch, steps),
        in_specs=[tile] + [whole(a) for a in operands],
        out_specs=out_specs,
        out_shape=out_shape,
        scratch_shapes=scratch_shapes(tm, seq),
        compiler_params=pltpu.CompilerParams(
            dimension_semantics=("arbitrary", "arbitrary"),
            vmem_limit_bytes=V7X_VMEM_LIMIT_BYTES),
        name=name,
    )(x, *operands)


def _sb_layer(x, norm_gain, w_in, q_gain, k_gain, w_out):
    width = w_out.shape[0]
    n_heads = width // HEAD_DIM
    assert w_in.shape[1] == 4 * width

    def scratch(tm, seq):
        return [
            pltpu.VMEM((tm, x.shape[-1]), _BF16),
            pltpu.VMEM((n_heads, tm, HEAD_DIM), _BF16),
            pltpu.VMEM((n_heads, seq, HEAD_DIM), _BF16),
            pltpu.VMEM((n_heads, seq, HEAD_DIM), _BF16),
            pltpu.VMEM((n_heads, tm, HEAD_DIM), _F32),
            pltpu.VMEM((n_heads, tm, HEAD_DIM), _F32),
            pltpu.VMEM((n_heads, tm, HEAD_DIM), _F32),
        ]

    operands = (norm_gain.reshape(1, -1), w_in.astype(_BF16), q_gain.reshape(1, -1),
                k_gain.reshape(1, -1), w_out.astype(_BF16))

    def run(windowed):
        return _layer_call(
            functools.partial(_sb_layer_kernel, n_heads=n_heads, windowed=windowed),
            x, operands, scratch,
            "stick_breaking_window" if windowed else "stick_breaking_layer", windowed)

    out, margin = run(True)
    return lax.cond(jnp.min(margin) >= SB_DEAD_SKIP_MASS, lambda: out, lambda: run(False))


def _hg_layer(x, norm_gain, w_in, o_gain, w_out, lb_logits, layer):
    width = w_out.shape[0]
    n_heads = width // HEAD_DIM
    assert w_in.shape[1] == 4 * width

    def scratch(tm, seq):
        per_head = lambda dt: pltpu.VMEM((n_heads, tm, HEAD_DIM), dt)
        return ([pltpu.VMEM((tm, x.shape[-1]), _BF16)] + [per_head(_F32)] * 6
                + [pltpu.VMEM((n_heads, HEAD_DIM, HEAD_DIM), _F32)])

    operands = (norm_gain.reshape(1, -1), w_in.astype(_BF16), o_gain.reshape(1, -1),
                w_out.astype(_BF16), lb_logits)

    def run(guarded):
        return _layer_call(
            functools.partial(_hg_layer_kernel, n_heads=n_heads, layer=layer, guarded=guarded),
            x, operands, scratch, "hgrn2_layer" if guarded else "hgrn2_matmul_path",
            not guarded)

    out, margin = run(False)
    return lax.cond(jnp.min(margin) >= -HG_MAX_CHUNK_DECAY, lambda: out, lambda: run(True))


def kernel(x, sb_norm, sb_w_in, sb_q_gain, sb_k_gain, sb_w_out,
           hg_norm, hg_w_in, hg_o_gain, hg_w_out, hg_lb_logits):
    depth = hg_lb_logits.shape[0]
    h = x
    for layer in range(depth):
        j = layer // 2
        if layer % 2 == 0:
            h = _sb_layer(h, sb_norm[j], sb_w_in[j], sb_q_gain[j], sb_k_gain[j], sb_w_out[j])
        else:
            h = _hg_layer(h, hg_norm[j], hg_w_in[j], hg_o_gain[j], hg_w_out[j],
                          hg_lb_logits, layer)
    return h
```

```python
import functools

import jax
import jax.numpy as jnp
from jax import lax
from jax.experimental import pallas as pl
from jax.experimental.pallas import tpu as pltpu

HEAD_DIM = 128
RMS_EPS = 1e-6
LOG2_E = 1.4426950408889634
ROW_TILE = 512
NORM_ROWS = 32
SB_BLOCK = 256
SB_SOFTPLUS_CAP = 64.0
SB_DEAD_SKIP_MASS = 160.0
HG_CHUNK = 128
HG_SUBCHUNK = 8
HG_MAX_CHUNK_DECAY = 80.0
SB_PIPELINE_SKEW = 2
HG_PIPELINE_SKEW = 4
MARGIN_ROWS = 8
NO_CONSTRAINT = 1e30
V7X_VMEM_LIMIT_BYTES = 56 * 1024 * 1024

_F32 = jnp.float32
_BF16 = jnp.bfloat16
_NT = (((1,), (1,)), ((), ()))
_TN = (((0,), (0,)), ((), ()))


def _rms_rows(x, gain):
    return x * lax.rsqrt(jnp.mean(x * x, axis=-1, keepdims=True) + RMS_EPS) * gain


def _silu(x):
    half = 0.5 * x
    return half + half * jnp.tanh(half)


def _normalise(x_ref, gain, u_scr):
    for r in range(0, u_scr.shape[0], NORM_ROWS):
        u_scr[r:r + NORM_ROWS, :] = _rms_rows(x_ref[0, r:r + NORM_ROWS, :], gain).astype(_BF16)


def _project(u_scr, win_ref, width, group_order, store):
    u = u_scr[...]
    n_heads = width // HEAD_DIM

    def matmul(grp):
        return jnp.dot(u, win_ref[:, grp * width:(grp + 1) * width],
                       preferred_element_type=_F32)

    ahead = matmul(group_order[0])
    for n, grp in enumerate(group_order):
        pg = ahead
        if n + 1 < len(group_order):
            ahead = matmul(group_order[n + 1])
        for h in range(n_heads):
            store(grp, h, pg[:, h * HEAD_DIM:(h + 1) * HEAD_DIM])


def _gated_out(x_ref, out_ref, o_scr, gate_scr, wout_ref):
    o = jnp.concatenate([o_scr[h] * gate_scr[h] for h in range(o_scr.shape[0])], axis=1)
    out_ref[0] = x_ref[0] + jnp.dot(o.astype(_BF16), wout_ref[...], preferred_element_type=_F32)


def _least_rows(x):
    least = x[:MARGIN_ROWS]
    for r in range(MARGIN_ROWS, x.shape[0], MARGIN_ROWS):
        least = jnp.minimum(least, x[r:r + MARGIN_ROWS])
    return least


def _software_pipeline(items, first, second, third, skew):
    a_out, b_out = {}, {}
    for step in range(len(items) + 2 * skew):
        n1, n2, n3 = step, step - skew, step - 2 * skew
        if n1 < len(items):
            a_out[n1] = first(*items[n1])
        if 0 <= n2 < len(items):
            b_out[n2] = second(*items[n2], *a_out.pop(n2))
        if 0 <= n3 < len(items):
            third(*items[n3], *b_out.pop(n3))


def _sb_layer_kernel(x_ref, ng_ref, win_ref, qg_ref, kg_ref, wout_ref, *rest,
                     n_heads, windowed):
    if windowed:
        out_ref, margin_ref = rest[:2]
    else:
        out_ref, margin_ref = rest[0], None
    u_scr, q_scr, k_scr, v_scr, gate_scr, acc_scr, sum_scr = rest[-7:]
    tm = x_ref.shape[1]
    blk = SB_BLOCK
    n_sub = tm // blk
    i = pl.program_id(1)
    row0 = pl.multiple_of(i * tm, tm)
    qg = qg_ref[...] * (HEAD_DIM ** -0.5 * LOG2_E)
    kg = kg_ref[...]

    def store(grp, h, tile):
        if grp == 0:
            q_scr[h] = _rms_rows(tile, qg).astype(_BF16)
        elif grp == 1:
            k_scr[h, pl.ds(row0, tm), :] = _rms_rows(tile, kg).astype(_BF16)
        elif grp == 2:
            v_scr[h, pl.ds(row0, tm), :] = tile.astype(_BF16)
        else:
            gate_scr[h] = _silu(tile)

    _normalise(x_ref, ng_ref[...], u_scr)
    _project(u_scr, win_ref, n_heads * HEAD_DIM, (0, 1, 3, 2), store)

    t_idx = lax.broadcasted_iota(jnp.int32, (blk, blk), 0)
    s_idx = lax.broadcasted_iota(jnp.int32, (blk, blk), 1)
    causal = s_idx < t_idx
    later_keys = jnp.where(causal, 1.0, 0.0).astype(_BF16)
    has_earlier_tiles = i > 0


    def stack(top, rest):
        return jnp.concatenate([top, rest], axis=0) if rest.shape[0] else top

    def gates(h, lo, n_rows, own, r0):
        q = q_scr[h, lo:lo + n_rows, :]
        z2 = lax.dot_general(q, k_scr[h, pl.ds(r0, blk), :], _NT, preferred_element_type=_F32)
        skip = jnp.maximum(
            z2, jnp.log(1.0 + jnp.exp2(jnp.minimum(z2, SB_SOFTPLUS_CAP))) * LOG2_E)
        log_beta = z2 - skip
        if own:
            skip = stack(jnp.where(causal, skip[:own], 0.0), skip[own:])
        row_sum = jnp.sum(skip, axis=-1, keepdims=True)
        return log_beta, skip.astype(_BF16), row_sum

    def weights(h, lo, n_rows, own, r0, log_beta, skip, row_sum):
        skipped = jnp.dot(skip, later_keys, preferred_element_type=_F32)
        parts = []
        if own:
            sum_scr[h, lo:lo + own, :] = jnp.broadcast_to(row_sum[:own], (own, HEAD_DIM))
            parts.append(jnp.where(causal, jnp.exp2(log_beta[:own] - skipped[:own]), 0.0))
        if own < n_rows:
            rest = slice(lo + own, lo + n_rows)
            nearer = sum_scr[h, rest, :]
            sum_scr[h, rest, :] = nearer + row_sum[own:]
            parts.append(jnp.exp2(log_beta[own:] - skipped[own:]
                                  - jnp.concatenate([nearer] * (blk // HEAD_DIM), axis=1)))
        return (jnp.concatenate(parts, axis=0).astype(_BF16),)

    def accumulate(h, lo, n_rows, own, r0, w):
        o = jnp.dot(w, v_scr[h, pl.ds(r0, blk), :], preferred_element_type=_F32)
        if own:
            acc_scr[h, lo:lo + own, :] = o[:own]
        if own < n_rows:
            acc_scr[h, lo + own:lo + n_rows, :] += o[own:]

    def least_skip_mass(a):
        rows = slice(a * blk, (a + 1) * blk)
        least = sum_scr[0, rows, :]
        for h in range(1, n_heads):
            least = jnp.minimum(least, sum_scr[h, rows, :])
        return least

    in_tile = [(h, kb * blk, tm - kb * blk, blk, row0 + kb * blk)
               for kb in range(n_sub - 1, -1, -1) for h in range(n_heads)]


    if windowed:
        def window_with_before():
            before = [(h, 0, blk, 0, pl.multiple_of(row0 - blk, blk)) for h in range(n_heads)]
            _software_pipeline(in_tile + before, gates, weights, accumulate, SB_PIPELINE_SKEW)

        def window_of_first_tile():
            _software_pipeline(in_tile, gates, weights, accumulate, SB_PIPELINE_SKEW)

        lax.cond(has_earlier_tiles, window_with_before, window_of_first_tile)
        least = least_skip_mass(0)
        for a in range(1, n_sub):
            least = jnp.minimum(least, least_skip_mass(a))
        margin_ref[0] = jnp.where(has_earlier_tiles, _least_rows(least), NO_CONSTRAINT)
    else:
        _software_pipeline(in_tile, gates, weights, accumulate, SB_PIPELINE_SKEW)
        for a in range(n_sub):
            def kv_cond(state):
                n, least = state
                return jnp.logical_and(n < n_sub * i, least < SB_DEAD_SKIP_MASS)

            def kv_body(state, a=a):
                n, _ = state
                r0 = pl.multiple_of(row0 - (n + 1) * blk, blk)
                earlier = [(h, a * blk, blk, 0, r0) for h in range(n_heads)]
                _software_pipeline(earlier, gates, weights, accumulate, SB_PIPELINE_SKEW)
                return n + 1, jnp.min(least_skip_mass(a))

            least = jnp.min(least_skip_mass(a)) if a > 0 else jnp.float32(0.0)
            lax.while_loop(kv_cond, kv_body, (jnp.int32(0), least))

    _gated_out(x_ref, out_ref, acc_scr, gate_scr, wout_ref)


def _hg_layer_kernel(x_ref, ng_ref, win_ref, og_ref, wout_ref, lbl_ref, *rest,
                     n_heads, layer, guarded):
    if guarded:
        out_ref, margin_ref = rest[0], None
    else:
        out_ref, margin_ref = rest[:2]
    u_scr, q_scr, g_scr, k_scr, i_scr, gate_scr, o_scr, st_scr = rest[-8:]
    tm = x_ref.shape[1]
    n_chunks = tm // HG_CHUNK
    og = og_ref[...]

    @pl.when(pl.program_id(1) == 0)
    def _():
        st_scr[...] = jnp.zeros_like(st_scr)

    lg = lbl_ref[...]
    lmax = jnp.max(lg, axis=0, keepdims=True)
    le = jnp.exp(lg - lmax)
    lb_row = jnp.sum(le[1:layer + 1], axis=0, keepdims=True) / jnp.sum(le, axis=0, keepdims=True)

    chunk_min = [None]

    def store(grp, h, tile):
        if grp == 0:
            q_scr[h] = _silu(tile)
        elif grp == 1:
            lb = lb_row[:, h * HEAD_DIM:(h + 1) * HEAD_DIM]
            half_span = 0.5 * (1.0 - lb)
            f = (lb + half_span) + half_span * jnp.tanh(0.5 * tile)
            f = jnp.maximum(f, lb)
            g = jnp.log(f)
            g_scr[h] = g
            k_scr[h] = 1.0 - f
            for c in range(n_chunks):
                tot = jnp.sum(g[c * HG_CHUNK:(c + 1) * HG_CHUNK], axis=0, keepdims=True)
                chunk_min[0] = tot if chunk_min[0] is None else jnp.minimum(chunk_min[0], tot)
        elif grp == 2:
            i_scr[h] = tile
        else:
            gate_scr[h] = _silu(tile)

    _normalise(x_ref, ng_ref[...], u_scr)
    _project(u_scr, win_ref, n_heads * HEAD_DIM, (1, 0, 3, 2), store)
    least_chunk_decay = chunk_min[0]

    def matmul_path():
        c_t = lax.broadcasted_iota(jnp.int32, (HG_CHUNK, HG_CHUNK), 0)
        c_s = lax.broadcasted_iota(jnp.int32, (HG_CHUNK, HG_CHUNK), 1)
        incl = c_s <= c_t
        earlier_rows = jnp.where(incl, 1.0, 0.0).astype(_BF16)


        def decays(c, h):
            rows = pl.ds(c * HG_CHUNK, HG_CHUNK)
            q = q_scr[h, rows, :]
            k = k_scr[h, rows, :]
            g = g_scr[h, rows, :]
            gc = jnp.dot(earlier_rows, g.astype(_BF16), preferred_element_type=_F32)
            g_last = gc[HG_CHUNK - 1:HG_CHUNK, :]
            g_mid = gc[HG_CHUNK // 2 - 1:HG_CHUNK // 2, :]
            to_mid = jnp.exp(gc - g_mid)
            q_mid = q * to_mid
            k_mid = k / to_mid
            q_in = q_mid * jnp.exp(g_mid)
            k_end = k_mid * jnp.exp(g_last - g_mid)
            return (q_mid.astype(_BF16), k_mid.astype(_BF16), q_in.astype(_BF16),
                    k_end.astype(_BF16), jnp.exp(g_last))

        def intra_scores(c, h, q_mid, k_mid, q_in, k_end, decay_last):
            scores = lax.dot_general(q_mid, k_mid, _NT, preferred_element_type=_F32)
            return jnp.where(incl, scores, 0.0).astype(_BF16), q_in, k_end, decay_last

        def outputs(c, h, scores, q_in, k_end, decay_last):
            rows = pl.ds(c * HG_CHUNK, HG_CHUNK)
            iv = i_scr[h, rows, :].astype(_BF16)
            st = st_scr[h]
            o = jnp.dot(scores, iv, preferred_element_type=_F32)
            o = o + lax.dot_general(q_in, st.astype(_BF16), _NT, preferred_element_type=_F32)
            st_scr[h] = decay_last * st + lax.dot_general(
                iv, k_end, _TN, preferred_element_type=_F32)
            o_scr[h, rows, :] = _rms_rows(o, og)

        items = [(c, h) for c in range(n_chunks) for h in range(n_heads)]
        _software_pipeline(items, decays, intra_scores, outputs, HG_PIPELINE_SKEW)

    def pairwise_path():
        sc = HG_SUBCHUNK
        r_t = lax.broadcasted_iota(jnp.int32, (sc, HEAD_DIM), 0)

        def head_body(h, carry_unused):
            def sub_body(c, st):
                rows = pl.ds(pl.multiple_of(c * sc, sc), sc)
                q = q_scr[h, rows, :]
                k = k_scr[h, rows, :]
                iv = i_scr[h, rows, :]
                g = g_scr[h, rows, :]
                gc = g
                for shift in (1, 2, 4):
                    moved = pltpu.roll(gc, shift, 0)
                    gc = gc + jnp.where(r_t >= shift, moved, 0.0)
                g_last = gc[sc - 1:sc, :]
                o = lax.dot_general((q * jnp.exp(gc)).astype(_BF16), st.astype(_BF16),
                                    _NT, preferred_element_type=_F32)
                for s in range(sc):
                    decay = jnp.exp(jnp.minimum(gc - gc[s:s + 1, :], 0.0))
                    pair = jnp.sum(q * k[s:s + 1, :] * decay, axis=-1, keepdims=True)
                    o = o + jnp.where(r_t >= s, pair, 0.0) * iv[s:s + 1, :]
                k_end = (k * jnp.exp(g_last - gc)).astype(_BF16)
                st = jnp.exp(g_last) * st + lax.dot_general(
                    iv.astype(_BF16), k_end, _TN, preferred_element_type=_F32)
                o_scr[h, rows, :] = _rms_rows(o, og)
                return st

            st_scr[h] = lax.fori_loop(0, tm // sc, sub_body, st_scr[h])
            return carry_unused

        lax.fori_loop(0, n_heads, head_body, 0)

    if guarded:
        lax.cond(jnp.min(least_chunk_decay) >= -HG_MAX_CHUNK_DECAY, matmul_path, pairwise_path)
    else:
        margin_ref[0] = jnp.broadcast_to(least_chunk_decay, (MARGIN_ROWS, HEAD_DIM))
        matmul_path()
    _gated_out(x_ref, out_ref, o_scr, gate_scr, wout_ref)


def _layer_call(body, x, operands, scratch_shapes, name, with_margin):
    batch, seq, d_model = x.shape
    tm = min(ROW_TILE, seq)
    assert seq % tm == 0 and tm % SB_BLOCK == 0 and tm % HG_CHUNK == 0
    steps = seq // tm

    def whole(a):
        return pl.BlockSpec(a.shape, lambda b, t: (0,) * a.ndim)

    tile = pl.BlockSpec((1, tm, d_model), lambda b, t: (b, t, 0))
    out_specs, out_shape = tile, jax.ShapeDtypeStruct(x.shape, x.dtype)
    if with_margin:
        out_specs = (tile, pl.BlockSpec((1, MARGIN_ROWS, HEAD_DIM),
                                        lambda b, t: (b * steps + t, 0, 0)))
        out_shape = (out_shape,
                     jax.ShapeDtypeStruct((batch * steps, MARGIN_ROWS, HEAD_DIM), _F32))
    return pl.pallas_call(
        body,
        grid=(batch, steps),
        in_specs=[tile] + [whole(a) for a in operands],
        out_specs=out_specs,
        out_shape=out_shape,
        scratch_shapes=scratch_shapes(tm, seq),
        compiler_params=pltpu.CompilerParams(
            dimension_semantics=("arbitrary", "arbitrary"),
            vmem_limit_bytes=V7X_VMEM_LIMIT_BYTES),
        name=name,
    )(x, *operands)


def _sb_layer(x, norm_gain, w_in, q_gain, k_gain, w_out):
    width = w_out.shape[0]
    n_heads = width // HEAD_DIM
    assert w_in.shape[1] == 4 * width

    def scratch(tm, seq):
        return [
            pltpu.VMEM((tm, x.shape[-1]), _BF16),
            pltpu.VMEM((n_heads, tm, HEAD_DIM), _BF16),
            pltpu.VMEM((n_heads, seq, HEAD_DIM), _BF16),
            pltpu.VMEM((n_heads, seq, HEAD_DIM), _BF16),
            pltpu.VMEM((n_heads, tm, HEAD_DIM), _F32),
            pltpu.VMEM((n_heads, tm, HEAD_DIM), _F32),
            pltpu.VMEM((n_heads, tm, HEAD_DIM), _F32),
        ]

    operands = (norm_gain.reshape(1, -1), w_in.astype(_BF16), q_gain.reshape(1, -1),
                k_gain.reshape(1, -1), w_out.astype(_BF16))

    def run(windowed):
        return _layer_call(
            functools.partial(_sb_layer_kernel, n_heads=n_heads, windowed=windowed),
            x, operands, scratch,
            "stick_breaking_window" if windowed else "stick_breaking_layer", windowed)

    out, margin = run(True)
    return lax.cond(jnp.min(margin) >= SB_DEAD_SKIP_MASS, lambda: out, lambda: run(False))


def _hg_layer(x, norm_gain, w_in, o_gain, w_out, lb_logits, layer):
    width = w_out.shape[0]
    n_heads = width // HEAD_DIM
    assert w_in.shape[1] == 4 * width

    def scratch(tm, seq):
        per_head = lambda dt: pltpu.VMEM((n_heads, tm, HEAD_DIM), dt)
        return ([pltpu.VMEM((tm, x.shape[-1]), _BF16)] + [per_head(_F32)] * 6
                + [pltpu.VMEM((n_heads, HEAD_DIM, HEAD_DIM), _F32)])

    operands = (norm_gain.reshape(1, -1), w_in.astype(_BF16), o_gain.reshape(1, -1),
                w_out.astype(_BF16), lb_logits)

    def run(guarded):
        return _layer_call(
            functools.partial(_hg_layer_kernel, n_heads=n_heads, layer=layer, guarded=guarded),
            x, operands, scratch, "hgrn2_layer" if guarded else "hgrn2_matmul_path",
            not guarded)

    out, margin = run(False)
    return lax.cond(jnp.min(margin) >= -HG_MAX_CHUNK_DECAY, lambda: out, lambda: run(True))


def kernel(x, sb_norm, sb_w_in, sb_q_gain, sb_k_gain, sb_w_out,
           hg_norm, hg_w_in, hg_o_gain, hg_w_out, hg_lb_logits):
    depth = hg_lb_logits.shape[0]
    h = x
    for layer in range(depth):
        j = layer // 2
        if layer % 2 == 0:
            h = _sb_layer(h, sb_norm[j], sb_w_in[j], sb_q_gain[j], sb_k_gain[j], sb_w_out[j])
        else:
            h = _hg_layer(h, hg_norm[j], hg_w_in[j], hg_o_gain[j], hg_w_out[j],
                          hg_lb_logits, layer)
    return h
```

```python
import functools

import jax
import jax.numpy as jnp
from jax import lax
from jax.experimental import pallas as pl
from jax.experimental.pallas import tpu as pltpu

HEAD_DIM = 128
RMS_EPS = 1e-6
LOG2_E = 1.4426950408889634
ROW_TILE = 512
NORM_ROWS = 32
SB_BLOCK = 256
SB_SOFTPLUS_CAP = 64.0
SB_DEAD_SKIP_MASS = 160.0
HG_CHUNK = 128
HG_SUBCHUNK = 8
HG_MAX_CHUNK_DECAY = 80.0
SB_PIPELINE_SKEW = 2
HG_PIPELINE_SKEW = 4
MARGIN_ROWS = 8
NO_CONSTRAINT = 1e30
V7X_VMEM_LIMIT_BYTES = 56 * 1024 * 1024

_F32 = jnp.float32
_BF16 = jnp.bfloat16
_NT = (((1,), (1,)), ((), ()))
_TN = (((0,), (0,)), ((), ()))


def _rms_rows(x, gain):
    return x * lax.rsqrt(jnp.mean(x * x, axis=-1, keepdims=True) + RMS_EPS) * gain


def _silu(x):
    half = 0.5 * x
    return half + half * jnp.tanh(half)


def _normalise(x_ref, gain, u_scr):
    for r in range(0, u_scr.shape[0], NORM_ROWS):
        u_scr[r:r + NORM_ROWS, :] = _rms_rows(x_ref[0, r:r + NORM_ROWS, :], gain).astype(_BF16)


def _project(u_scr, win_ref, width, group_order, store):
    u = u_scr[...]
    n_heads = width // HEAD_DIM

    def matmul(grp):
        return jnp.dot(u, win_ref[:, grp * width:(grp + 1) * width],
                       preferred_element_type=_F32)

    ahead = matmul(group_order[0])
    for n, grp in enumerate(group_order):
        pg = ahead
        if n + 1 < len(group_order):
            ahead = matmul(group_order[n + 1])
        for h in range(n_heads):
            store(grp, h, pg[:, h * HEAD_DIM:(h + 1) * HEAD_DIM])


def _gated_out(x_ref, out_ref, o_scr, gate_scr, wout_ref):
    o = jnp.concatenate([o_scr[h] * gate_scr[h] for h in range(o_scr.shape[0])], axis=1)
    out_ref[0] = x_ref[0] + jnp.dot(o.astype(_BF16), wout_ref[...], preferred_element_type=_F32)


def _least_rows(x):
    least = x[:MARGIN_ROWS]
    for r in range(MARGIN_ROWS, x.shape[0], MARGIN_ROWS):
        least = jnp.minimum(least, x[r:r + MARGIN_ROWS])
    return least


def _software_pipeline(items, first, second, third, skew):
    a_out, b_out = {}, {}
    for step in range(len(items) + 2 * skew):
        n1, n2, n3 = step, step - skew, step - 2 * skew
        if n1 < len(items):
            a_out[n1] = first(*items[n1])
        if 0 <= n2 < len(items):
            b_out[n2] = second(*items[n2], *a_out.pop(n2))
        if 0 <= n3 < len(items):
            third(*items[n3], *b_out.pop(n3))


def _sb_layer_kernel(x_ref, ng_ref, win_ref, qg_ref, kg_ref, wout_ref, *rest,
                     n_heads, windowed):
    if windowed:
        out_ref, margin_ref = rest[:2]
    else:
        out_ref, margin_ref = rest[0], None
    u_scr, q_scr, k_scr, v_scr, gate_scr, acc_scr, sum_scr = rest[-7:]
    tm = x_ref.shape[1]
    blk = SB_BLOCK
    n_sub = tm // blk
    i = pl.program_id(1)
    row0 = pl.multiple_of(i * tm, tm)
    qg = qg_ref[...] * (HEAD_DIM ** -0.5 * LOG2_E)
    kg = kg_ref[...]

    def store(grp, h, tile):
        if grp == 0:
            q_scr[h] = _rms_rows(tile, qg).astype(_BF16)
        elif grp == 1:
            k_scr[h, pl.ds(row0, tm), :] = _rms_rows(tile, kg).astype(_BF16)
        elif grp == 2:
            v_scr[h, pl.ds(row0, tm), :] = tile.astype(_BF16)
        else:
            gate_scr[h] = _silu(tile)

    _normalise(x_ref, ng_ref[...], u_scr)
    _project(u_scr, win_ref, n_heads * HEAD_DIM, (0, 1, 3, 2), store)

    t_idx = lax.broadcasted_iota(jnp.int32, (blk, blk), 0)
    s_idx = lax.broadcasted_iota(jnp.int32, (blk, blk), 1)
    causal = s_idx < t_idx
    later_keys = jnp.where(causal, 1.0, 0.0).astype(_BF16)
    has_earlier_tiles = i > 0


    def stack(top, rest):
        return jnp.concatenate([top, rest], axis=0) if rest.shape[0] else top

    def gates(h, lo, n_rows, own, r0):
        q = q_scr[h, lo:lo + n_rows, :]
        z2 = lax.dot_general(q, k_scr[h, pl.ds(r0, blk), :], _NT, preferred_element_type=_F32)
        skip = jnp.maximum(
            z2, jnp.log(1.0 + jnp.exp2(jnp.minimum(z2, SB_SOFTPLUS_CAP))) * LOG2_E)
        log_beta = z2 - skip
        if own:
            skip = stack(jnp.where(causal, skip[:own], 0.0), skip[own:])
        row_sum = jnp.sum(skip, axis=-1, keepdims=True)
        return log_beta, skip.astype(_BF16), row_sum

    def weights(h, lo, n_rows, own, r0, log_beta, skip, row_sum):
        skipped = jnp.dot(skip, later_keys, preferred_element_type=_F32)
        parts = []
        if own:
            sum_scr[h, lo:lo + own, :] = jnp.broadcast_to(row_sum[:own], (own, HEAD_DIM))
            parts.append(jnp.where(causal, jnp.exp2(log_beta[:own] - skipped[:own]), 0.0))
        if own < n_rows:
            rest = slice(lo + own, lo + n_rows)
            nearer = sum_scr[h, rest, :]
            sum_scr[h, rest, :] = nearer + row_sum[own:]
            parts.append(jnp.exp2(log_beta[own:] - skipped[own:]
                                  - jnp.concatenate([nearer] * (blk // HEAD_DIM), axis=1)))
        return (jnp.concatenate(parts, axis=0).astype(_BF16),)

    def accumulate(h, lo, n_rows, own, r0, w):
        o = jnp.dot(w, v_scr[h, pl.ds(r0, blk), :], preferred_element_type=_F32)
        if own:
            acc_scr[h, lo:lo + own, :] = o[:own]
        if own < n_rows:
            acc_scr[h, lo + own:lo + n_rows, :] += o[own:]

    def least_skip_mass(a):
        rows = slice(a * blk, (a + 1) * blk)
        least = sum_scr[0, rows, :]
        for h in range(1, n_heads):
            least = jnp.minimum(least, sum_scr[h, rows, :])
        return least

    in_tile = [(h, kb * blk, tm - kb * blk, blk, row0 + kb * blk)
               for kb in range(n_sub - 1, -1, -1) for h in range(n_heads)]


    if windowed:
        def window_with_before():
            before = [(h, 0, blk, 0, pl.multiple_of(row0 - blk, blk)) for h in range(n_heads)]
            _software_pipeline(in_tile + before, gates, weights, accumulate, SB_PIPELINE_SKEW)

        def window_of_first_tile():
            _software_pipeline(in_tile, gates, weights, accumulate, SB_PIPELINE_SKEW)

        lax.cond(has_earlier_tiles, window_with_before, window_of_first_tile)
        least = least_skip_mass(0)
        for a in range(1, n_sub):
            least = jnp.minimum(least, least_skip_mass(a))
        margin_ref[0] = jnp.where(has_earlier_tiles, _least_rows(least), NO_CONSTRAINT)
    else:
        _software_pipeline(in_tile, gates, weights, accumulate, SB_PIPELINE_SKEW)
        for a in range(n_sub):
            def kv_cond(state):
                n, least = state
                return jnp.logical_and(n < n_sub * i, least < SB_DEAD_SKIP_MASS)

            def kv_body(state, a=a):
                n, _ = state
                r0 = pl.multiple_of(row0 - (n + 1) * blk, blk)
                earlier = [(h, a * blk, blk, 0, r0) for h in range(n_heads)]
                _software_pipeline(earlier, gates, weights, accumulate, SB_PIPELINE_SKEW)
                return n + 1, jnp.min(least_skip_mass(a))

            least = jnp.min(least_skip_mass(a)) if a > 0 else jnp.float32(0.0)
            lax.while_loop(kv_cond, kv_body, (jnp.int32(0), least))

    _gated_out(x_ref, out_ref, acc_scr, gate_scr, wout_ref)


def _hg_layer_kernel(x_ref, ng_ref, win_ref, og_ref, wout_ref, lbl_ref, *rest,
                     n_heads, layer, guarded):
    if guarded:
        out_ref, margin_ref = rest[0], None
    else:
        out_ref, margin_ref = rest[:2]
    u_scr, q_scr, g_scr, k_scr, i_scr, gate_scr, o_scr, st_scr = rest[-8:]
    tm = x_ref.shape[1]
    n_chunks = tm // HG_CHUNK
    og = og_ref[...]

    @pl.when(pl.program_id(1) == 0)
    def _():
        st_scr[...] = jnp.zeros_like(st_scr)

    lg = lbl_ref[...]
    lmax = jnp.max(lg, axis=0, keepdims=True)
    le = jnp.exp(lg - lmax)
    lb_row = jnp.sum(le[1:layer + 1], axis=0, keepdims=True) / jnp.sum(le, axis=0, keepdims=True)

    chunk_min = [None]

    def store(grp, h, tile):
        if grp == 0:
            q_scr[h] = _silu(tile)
        elif grp == 1:
            lb = lb_row[:, h * HEAD_DIM:(h + 1) * HEAD_DIM]
            half_span = 0.5 * (1.0 - lb)
            f = (lb + half_span) + half_span * jnp.tanh(0.5 * tile)
            f = jnp.maximum(f, lb)
            g = jnp.log(f)
            g_scr[h] = g
            k_scr[h] = 1.0 - f
            for c in range(n_chunks):
                tot = jnp.sum(g[c * HG_CHUNK:(c + 1) * HG_CHUNK], axis=0, keepdims=True)
                chunk_min[0] = tot if chunk_min[0] is None else jnp.minimum(chunk_min[0], tot)
        elif grp == 2:
            i_scr[h] = tile
        else:
            gate_scr[h] = _silu(tile)

    _normalise(x_ref, ng_ref[...], u_scr)
    _project(u_scr, win_ref, n_heads * HEAD_DIM, (1, 0, 3, 2), store)
    least_chunk_decay = chunk_min[0]

    def matmul_path():
        c_t = lax.broadcasted_iota(jnp.int32, (HG_CHUNK, HG_CHUNK), 0)
        c_s = lax.broadcasted_iota(jnp.int32, (HG_CHUNK, HG_CHUNK), 1)
        incl = c_s <= c_t
        earlier_rows = jnp.where(incl, 1.0, 0.0).astype(_BF16)


        def decays(c, h):
            rows = pl.ds(c * HG_CHUNK, HG_CHUNK)
            q = q_scr[h, rows, :]
            k = k_scr[h, rows, :]
            g = g_scr[h, rows, :]
            gc = jnp.dot(earlier_rows, g.astype(_BF16), preferred_element_type=_F32)
            g_last = gc[HG_CHUNK - 1:HG_CHUNK, :]
            g_mid = gc[HG_CHUNK // 2 - 1:HG_CHUNK // 2, :]
            to_mid = jnp.exp(gc - g_mid)
            q_mid = q * to_mid
            k_mid = k / to_mid
            q_in = q_mid * jnp.exp(g_mid)
            k_end = k_mid * jnp.exp(g_last - g_mid)
            return (q_mid.astype(_BF16), k_mid.astype(_BF16), q_in.astype(_BF16),
                    k_end.astype(_BF16), jnp.exp(g_last))

        def intra_scores(c, h, q_mid, k_mid, q_in, k_end, decay_last):
            scores = lax.dot_general(q_mid, k_mid, _NT, preferred_element_type=_F32)
            return jnp.where(incl, scores, 0.0).astype(_BF16), q_in, k_end, decay_last

        def outputs(c, h, scores, q_in, k_end, decay_last):
            rows = pl.ds(c * HG_CHUNK, HG_CHUNK)
            iv = i_scr[h, rows, :].astype(_BF16)
            st = st_scr[h]
            o = jnp.dot(scores, iv, preferred_element_type=_F32)
            o = o + jnp.dot(q_in, st.T.astype(_BF16), preferred_element_type=_F32)
            st_scr[h] = decay_last * st + lax.dot_general(
                iv, k_end, _TN, preferred_element_type=_F32)
            o_scr[h, rows, :] = _rms_rows(o, og)

        items = [(c, h) for c in range(n_chunks) for h in range(n_heads)]
        _software_pipeline(items, decays, intra_scores, outputs, HG_PIPELINE_SKEW)

    def pairwise_path():
        sc = HG_SUBCHUNK
        r_t = lax.broadcasted_iota(jnp.int32, (sc, HEAD_DIM), 0)

        def head_body(h, carry_unused):
            def sub_body(c, st):
                rows = pl.ds(pl.multiple_of(c * sc, sc), sc)
                q = q_scr[h, rows, :]
                k = k_scr[h, rows, :]
                iv = i_scr[h, rows, :]
                g = g_scr[h, rows, :]
                gc = g
                for shift in (1, 2, 4):
                    moved = pltpu.roll(gc, shift, 0)
                    gc = gc + jnp.where(r_t >= shift, moved, 0.0)
                g_last = gc[sc - 1:sc, :]
                o = lax.dot_general((q * jnp.exp(gc)).astype(_BF16), st.astype(_BF16),
                                    _NT, preferred_element_type=_F32)
                for s in range(sc):
                    decay = jnp.exp(jnp.minimum(gc - gc[s:s + 1, :], 0.0))
                    pair = jnp.sum(q * k[s:s + 1, :] * decay, axis=-1, keepdims=True)
                    o = o + jnp.where(r_t >= s, pair, 0.0) * iv[s:s + 1, :]
                k_end = (k * jnp.exp(g_last - gc)).astype(_BF16)
                st = jnp.exp(g_last) * st + lax.dot_general(
                    iv.astype(_BF16), k_end, _TN, preferred_element_type=_F32)
                o_scr[h, rows, :] = _rms_rows(o, og)
                return st

            st_scr[h] = lax.fori_loop(0, tm // sc, sub_body, st_scr[h])
            return carry_unused

        lax.fori_loop(0, n_heads, head_body, 0)

    if guarded:
        lax.cond(jnp.min(least_chunk_decay) >= -HG_MAX_CHUNK_DECAY, matmul_path, pairwise_path)
    else:
        margin_ref[0] = jnp.broadcast_to(least_chunk_decay, (MARGIN_ROWS, HEAD_DIM))
        matmul_path()
    _gated_out(x_ref, out_ref, o_scr, gate_scr, wout_ref)


def _layer_call(body, x, operands, scratch_shapes, name, with_margin):
    batch, seq, d_model = x.shape
    tm = min(ROW_TILE, seq)
    assert seq % tm == 0 and tm % SB_BLOCK == 0 and tm % HG_CHUNK == 0
    steps = seq // tm

    def whole(a):
        return pl.BlockSpec(a.shape, lambda b, t: (0,) * a.ndim)

    tile = pl.BlockSpec((1, tm, d_model), lambda b, t: (b, t, 0))
    out_specs, out_shape = tile, jax.ShapeDtypeStruct(x.shape, x.dtype)
    if with_margin:
        out_specs = (tile, pl.BlockSpec((1, MARGIN_ROWS, HEAD_DIM),
                                        lambda b, t: (b * steps + t, 0, 0)))
        out_shape = (out_shape,
                     jax.ShapeDtypeStruct((batch * steps, MARGIN_ROWS, HEAD_DIM), _F32))
    return pl.pallas_call(
        body,
        grid=(batch, steps),
        in_specs=[tile] + [whole(a) for a in operands],
        out_specs=out_specs,
        out_shape=out_shape,
        scratch_shapes=scratch_shapes(tm, seq),
        compiler_params=pltpu.CompilerParams(
            dimension_semantics=("arbitrary", "arbitrary"),
            vmem_limit_bytes=V7X_VMEM_LIMIT_BYTES),
        name=name,
    )(x, *operands)


def _sb_layer(x, norm_gain, w_in, q_gain, k_gain, w_out):
    width = w_out.shape[0]
    n_heads = width // HEAD_DIM
    assert w_in.shape[1] == 4 * width

    def scratch(tm, seq):
        return [
            pltpu.VMEM((tm, x.shape[-1]), _BF16),
            pltpu.VMEM((n_heads, tm, HEAD_DIM), _BF16),
            pltpu.VMEM((n_heads, seq, HEAD_DIM), _BF16),
            pltpu.VMEM((n_heads, seq, HEAD_DIM), _BF16),
            pltpu.VMEM((n_heads, tm, HEAD_DIM), _F32),
            pltpu.VMEM((n_heads, tm, HEAD_DIM), _F32),
            pltpu.VMEM((n_heads, tm, HEAD_DIM), _F32),
        ]

    operands = (norm_gain.reshape(1, -1), w_in.astype(_BF16), q_gain.reshape(1, -1),
                k_gain.reshape(1, -1), w_out.astype(_BF16))

    def run(windowed):
        return _layer_call(
            functools.partial(_sb_layer_kernel, n_heads=n_heads, windowed=windowed),
            x, operands, scratch,
            "stick_breaking_window" if windowed else "stick_breaking_layer", windowed)

    out, margin = run(True)
    return lax.cond(jnp.min(margin) >= SB_DEAD_SKIP_MASS, lambda: out, lambda: run(False))


def _hg_layer(x, norm_gain, w_in, o_gain, w_out, lb_logits, layer):
    width = w_out.shape[0]
    n_heads = width // HEAD_DIM
    assert w_in.shape[1] == 4 * width

    def scratch(tm, seq):
        per_head = lambda dt: pltpu.VMEM((n_heads, tm, HEAD_DIM), dt)
        return ([pltpu.VMEM((tm, x.shape[-1]), _BF16)] + [per_head(_F32)] * 6
                + [pltpu.VMEM((n_heads, HEAD_DIM, HEAD_DIM), _F32)])

    operands = (norm_gain.reshape(1, -1), w_in.astype(_BF16), o_gain.reshape(1, -1),
                w_out.astype(_BF16), lb_logits)

    def run(guarded):
        return _layer_call(
            functools.partial(_hg_layer_kernel, n_heads=n_heads, layer=layer, guarded=guarded),
            x, operands, scratch, "hgrn2_layer" if guarded else "hgrn2_matmul_path",
            not guarded)

    out, margin = run(False)
    return lax.cond(jnp.min(margin) >= -HG_MAX_CHUNK_DECAY, lambda: out, lambda: run(True))


def kernel(x, sb_norm, sb_w_in, sb_q_gain, sb_k_gain, sb_w_out,
           hg_norm, hg_w_in, hg_o_gain, hg_w_out, hg_lb_logits):
    depth = hg_lb_logits.shape[0]
    h = x
    for layer in range(depth):
        j = layer // 2
        if layer % 2 == 0:
            h = _sb_layer(h, sb_norm[j], sb_w_in[j], sb_q_gain[j], sb_k_gain[j], sb_w_out[j])
        else:
            h = _hg_layer(h, hg_norm[j], hg_w_in[j], hg_o_gain[j], hg_w_out[j],
                          hg_lb_logits, layer)
    return h
```

```python
import functools

import jax
import jax.numpy as jnp
from jax import lax
from jax.experimental import pallas as pl
from jax.experimental.pallas import tpu as pltpu

HEAD_DIM = 128
RMS_EPS = 1e-6
LOG2_E = 1.4426950408889634
ROW_TILE = 512
HG_BRANCH_FREE_ROW_TILE = 1024
NORM_ROWS = 32
SB_BLOCK = 256
SB_SOFTPLUS_CAP = 64.0
SB_DEAD_SKIP_MASS = 160.0
HG_CHUNK = 128
HG_SUBCHUNK = 8
HG_MAX_CHUNK_DECAY = 80.0
SB_PIPELINE_SKEW = 2
HG_PIPELINE_SKEW = 4
MARGIN_ROWS = 8
NO_CONSTRAINT = 1e30
V7X_VMEM_LIMIT_BYTES = 56 * 1024 * 1024

_F32 = jnp.float32
_BF16 = jnp.bfloat16
_NT = (((1,), (1,)), ((), ()))
_TN = (((0,), (0,)), ((), ()))


def _rms_rows(x, gain):
    return x * lax.rsqrt(jnp.mean(x * x, axis=-1, keepdims=True) + RMS_EPS) * gain


def _silu(x):
    half = 0.5 * x
    return half + half * jnp.tanh(half)


def _normalise(x_ref, gain, u_scr):
    for r in range(0, u_scr.shape[0], NORM_ROWS):
        u_scr[r:r + NORM_ROWS, :] = _rms_rows(x_ref[0, r:r + NORM_ROWS, :], gain).astype(_BF16)


def _project(u_scr, win_ref, width, group_order, store):
    u = u_scr[...]
    n_heads = width // HEAD_DIM

    def matmul(grp):
        return jnp.dot(u, win_ref[:, grp * width:(grp + 1) * width],
                       preferred_element_type=_F32)

    ahead = matmul(group_order[0])
    for n, grp in enumerate(group_order):
        pg = ahead
        if n + 1 < len(group_order):
            ahead = matmul(group_order[n + 1])
        for h in range(n_heads):
            store(grp, h, pg[:, h * HEAD_DIM:(h + 1) * HEAD_DIM])


def _gated_out(x_ref, out_ref, o_scr, gate_scr, wout_ref):
    o = jnp.concatenate([o_scr[h] * gate_scr[h] for h in range(o_scr.shape[0])], axis=1)
    out_ref[0] = x_ref[0] + jnp.dot(o.astype(_BF16), wout_ref[...], preferred_element_type=_F32)


def _least_rows(x):
    least = x[:MARGIN_ROWS]
    for r in range(MARGIN_ROWS, x.shape[0], MARGIN_ROWS):
        least = jnp.minimum(least, x[r:r + MARGIN_ROWS])
    return least


def _software_pipeline(items, first, second, third, skew):
    a_out, b_out = {}, {}
    for step in range(len(items) + 2 * skew):
        n1, n2, n3 = step, step - skew, step - 2 * skew
        if n1 < len(items):
            a_out[n1] = first(*items[n1])
        if 0 <= n2 < len(items):
            b_out[n2] = second(*items[n2], *a_out.pop(n2))
        if 0 <= n3 < len(items):
            third(*items[n3], *b_out.pop(n3))


def _sb_layer_kernel(x_ref, ng_ref, win_ref, qg_ref, kg_ref, wout_ref, *rest,
                     n_heads, windowed):
    if windowed:
        out_ref, margin_ref = rest[:2]
    else:
        out_ref, margin_ref = rest[0], None
    u_scr, q_scr, k_scr, v_scr, gate_scr, acc_scr, sum_scr = rest[-7:]
    tm = x_ref.shape[1]
    blk = SB_BLOCK
    n_sub = tm // blk
    i = pl.program_id(1)
    row0 = pl.multiple_of(i * tm, tm)
    qg = qg_ref[...] * (HEAD_DIM ** -0.5 * LOG2_E)
    kg = kg_ref[...]

    def store(grp, h, tile):
        if grp == 0:
            q_scr[h] = _rms_rows(tile, qg).astype(_BF16)
        elif grp == 1:
            k_scr[h, pl.ds(row0, tm), :] = _rms_rows(tile, kg).astype(_BF16)
        elif grp == 2:
            v_scr[h, pl.ds(row0, tm), :] = tile.astype(_BF16)
        else:
            gate_scr[h] = _silu(tile)

    _normalise(x_ref, ng_ref[...], u_scr)
    _project(u_scr, win_ref, n_heads * HEAD_DIM, (0, 1, 3, 2), store)

    t_idx = lax.broadcasted_iota(jnp.int32, (blk, blk), 0)
    s_idx = lax.broadcasted_iota(jnp.int32, (blk, blk), 1)
    causal = s_idx < t_idx
    later_keys = jnp.where(causal, 1.0, 0.0).astype(_BF16)
    has_earlier_tiles = i > 0


    def stack(top, rest):
        return jnp.concatenate([top, rest], axis=0) if rest.shape[0] else top

    def gates(h, lo, n_rows, own, r0):
        q = q_scr[h, lo:lo + n_rows, :]
        z2 = lax.dot_general(q, k_scr[h, pl.ds(r0, blk), :], _NT, preferred_element_type=_F32)
        skip = jnp.maximum(
            z2, jnp.log(1.0 + jnp.exp2(jnp.minimum(z2, SB_SOFTPLUS_CAP))) * LOG2_E)
        log_beta = z2 - skip
        if own:
            skip = stack(jnp.where(causal, skip[:own], 0.0), skip[own:])
        row_sum = jnp.sum(skip, axis=-1, keepdims=True)
        return log_beta, skip.astype(_BF16), row_sum

    def weights(h, lo, n_rows, own, r0, log_beta, skip, row_sum):
        skipped = jnp.dot(skip, later_keys, preferred_element_type=_F32)
        parts = []
        if own:
            sum_scr[h, lo:lo + own, :] = jnp.broadcast_to(row_sum[:own], (own, HEAD_DIM))
            parts.append(jnp.where(causal, jnp.exp2(log_beta[:own] - skipped[:own]), 0.0))
        if own < n_rows:
            rest = slice(lo + own, lo + n_rows)
            nearer = sum_scr[h, rest, :]
            sum_scr[h, rest, :] = nearer + row_sum[own:]
            parts.append(jnp.exp2(log_beta[own:] - skipped[own:]
                                  - jnp.concatenate([nearer] * (blk // HEAD_DIM), axis=1)))
        return (jnp.concatenate(parts, axis=0).astype(_BF16),)

    def accumulate(h, lo, n_rows, own, r0, w):
        o = jnp.dot(w, v_scr[h, pl.ds(r0, blk), :], preferred_element_type=_F32)
        if own:
            acc_scr[h, lo:lo + own, :] = o[:own]
        if own < n_rows:
            acc_scr[h, lo + own:lo + n_rows, :] += o[own:]

    def least_skip_mass(a):
        rows = slice(a * blk, (a + 1) * blk)
        least = sum_scr[0, rows, :]
        for h in range(1, n_heads):
            least = jnp.minimum(least, sum_scr[h, rows, :])
        return least

    in_tile = [(h, kb * blk, tm - kb * blk, blk, row0 + kb * blk)
               for kb in range(n_sub - 1, -1, -1) for h in range(n_heads)]


    if windowed:
        def window_with_before():
            before = [(h, 0, blk, 0, pl.multiple_of(row0 - blk, blk)) for h in range(n_heads)]
            _software_pipeline(in_tile + before, gates, weights, accumulate, SB_PIPELINE_SKEW)

        def window_of_first_tile():
            _software_pipeline(in_tile, gates, weights, accumulate, SB_PIPELINE_SKEW)

        lax.cond(has_earlier_tiles, window_with_before, window_of_first_tile)
        least = least_skip_mass(0)
        for a in range(1, n_sub):
            least = jnp.minimum(least, least_skip_mass(a))
        margin_ref[0] = jnp.where(has_earlier_tiles, _least_rows(least), NO_CONSTRAINT)
    else:
        _software_pipeline(in_tile, gates, weights, accumulate, SB_PIPELINE_SKEW)
        for a in range(n_sub):
            def kv_cond(state):
                n, least = state
                return jnp.logical_and(n < n_sub * i, least < SB_DEAD_SKIP_MASS)

            def kv_body(state, a=a):
                n, _ = state
                r0 = pl.multiple_of(row0 - (n + 1) * blk, blk)
                earlier = [(h, a * blk, blk, 0, r0) for h in range(n_heads)]
                _software_pipeline(earlier, gates, weights, accumulate, SB_PIPELINE_SKEW)
                return n + 1, jnp.min(least_skip_mass(a))

            least = jnp.min(least_skip_mass(a)) if a > 0 else jnp.float32(0.0)
            lax.while_loop(kv_cond, kv_body, (jnp.int32(0), least))

    _gated_out(x_ref, out_ref, acc_scr, gate_scr, wout_ref)


def _hg_layer_kernel(x_ref, ng_ref, win_ref, og_ref, wout_ref, lbl_ref, *rest,
                     n_heads, layer, guarded):
    if guarded:
        out_ref, margin_ref = rest[0], None
    else:
        out_ref, margin_ref = rest[:2]
    u_scr, q_scr, g_scr, k_scr, i_scr, gate_scr, o_scr, st_scr = rest[-8:]
    tm = x_ref.shape[1]
    n_chunks = tm // HG_CHUNK
    og = og_ref[...]

    @pl.when(pl.program_id(1) == 0)
    def _():
        st_scr[...] = jnp.zeros_like(st_scr)

    lg = lbl_ref[...]
    lmax = jnp.max(lg, axis=0, keepdims=True)
    le = jnp.exp(lg - lmax)
    lb_row = jnp.sum(le[1:layer + 1], axis=0, keepdims=True) / jnp.sum(le, axis=0, keepdims=True)

    chunk_min = [None]

    def note_chunk_total(tot):
        chunk_min[0] = tot if chunk_min[0] is None else jnp.minimum(chunk_min[0], tot)

    def forget_gates(f_pre, h):
        lb = lb_row[:, h * HEAD_DIM:(h + 1) * HEAD_DIM]
        half_span = 0.5 * (1.0 - lb)
        f = (lb + half_span) + half_span * jnp.tanh(0.5 * f_pre)
        f = jnp.maximum(f, lb)
        return jnp.log(f), 1.0 - f

    def store(grp, h, tile):
        if grp == 0:
            q_scr[h] = _silu(tile)
        elif grp == 1 and guarded:
            g, k = forget_gates(tile, h)
            g_scr[h] = g
            k_scr[h] = k
            for c in range(n_chunks):
                note_chunk_total(jnp.sum(g[c * HG_CHUNK:(c + 1) * HG_CHUNK], axis=0, keepdims=True))
        elif grp == 1:
            g_scr[h] = tile
        elif grp == 2:
            i_scr[h] = tile
        else:
            gate_scr[h] = _silu(tile)

    _normalise(x_ref, ng_ref[...], u_scr)
    _project(u_scr, win_ref, n_heads * HEAD_DIM, (1, 0, 3, 2), store)

    def matmul_path():
        c_t = lax.broadcasted_iota(jnp.int32, (HG_CHUNK, HG_CHUNK), 0)
        c_s = lax.broadcasted_iota(jnp.int32, (HG_CHUNK, HG_CHUNK), 1)
        incl = c_s <= c_t
        earlier_rows = jnp.where(incl, 1.0, 0.0).astype(_BF16)


        def decays(c, h):
            rows = pl.ds(c * HG_CHUNK, HG_CHUNK)
            q = q_scr[h, rows, :]
            if guarded:
                g, k = g_scr[h, rows, :], k_scr[h, rows, :]
            else:
                g, k = forget_gates(g_scr[h, rows, :], h)
            gc = jnp.dot(earlier_rows, g.astype(_BF16), preferred_element_type=_F32)
            g_last = gc[HG_CHUNK - 1:HG_CHUNK, :]
            g_mid = gc[HG_CHUNK // 2 - 1:HG_CHUNK // 2, :]
            if not guarded:
                note_chunk_total(g_last)
            to_mid = jnp.exp(gc - g_mid)
            q_mid = q * to_mid
            k_mid = k / to_mid
            q_in = q_mid * jnp.exp(g_mid)
            k_end = k_mid * jnp.exp(g_last - g_mid)
            return (q_mid.astype(_BF16), k_mid.astype(_BF16), q_in.astype(_BF16),
                    k_end.astype(_BF16), jnp.exp(g_last))

        def intra_scores(c, h, q_mid, k_mid, q_in, k_end, decay_last):
            scores = lax.dot_general(q_mid, k_mid, _NT, preferred_element_type=_F32)
            return jnp.where(incl, scores, 0.0).astype(_BF16), q_in, k_end, decay_last

        def outputs(c, h, scores, q_in, k_end, decay_last):
            rows = pl.ds(c * HG_CHUNK, HG_CHUNK)
            iv = i_scr[h, rows, :].astype(_BF16)
            st = st_scr[h]
            o = jnp.dot(scores, iv, preferred_element_type=_F32)
            o = o + jnp.dot(q_in, st.T.astype(_BF16), preferred_element_type=_F32)
            st_scr[h] = decay_last * st + lax.dot_general(
                iv, k_end, _TN, preferred_element_type=_F32)
            o_scr[h, rows, :] = _rms_rows(o, og)

        items = [(c, h) for c in range(n_chunks) for h in range(n_heads)]
        _software_pipeline(items, decays, intra_scores, outputs, HG_PIPELINE_SKEW)

    def pairwise_path():
        sc = HG_SUBCHUNK
        r_t = lax.broadcasted_iota(jnp.int32, (sc, HEAD_DIM), 0)

        def head_body(h, carry_unused):
            def sub_body(c, st):
                rows = pl.ds(pl.multiple_of(c * sc, sc), sc)
                q = q_scr[h, rows, :]
                k = k_scr[h, rows, :]
                iv = i_scr[h, rows, :]
                g = g_scr[h, rows, :]
                gc = g
                for shift in (1, 2, 4):
                    moved = pltpu.roll(gc, shift, 0)
                    gc = gc + jnp.where(r_t >= shift, moved, 0.0)
                g_last = gc[sc - 1:sc, :]
                o = lax.dot_general((q * jnp.exp(gc)).astype(_BF16), st.astype(_BF16),
                                    _NT, preferred_element_type=_F32)
                for s in range(sc):
                    decay = jnp.exp(jnp.minimum(gc - gc[s:s + 1, :], 0.0))
                    pair = jnp.sum(q * k[s:s + 1, :] * decay, axis=-1, keepdims=True)
                    o = o + jnp.where(r_t >= s, pair, 0.0) * iv[s:s + 1, :]
                k_end = (k * jnp.exp(g_last - gc)).astype(_BF16)
                st = jnp.exp(g_last) * st + lax.dot_general(
                    iv.astype(_BF16), k_end, _TN, preferred_element_type=_F32)
                o_scr[h, rows, :] = _rms_rows(o, og)
                return st

            st_scr[h] = lax.fori_loop(0, tm // sc, sub_body, st_scr[h])
            return carry_unused

        lax.fori_loop(0, n_heads, head_body, 0)

    if guarded:
        lax.cond(jnp.min(chunk_min[0]) >= -HG_MAX_CHUNK_DECAY, matmul_path, pairwise_path)
    else:
        matmul_path()
        margin_ref[0] = jnp.broadcast_to(chunk_min[0], (MARGIN_ROWS, HEAD_DIM))
    _gated_out(x_ref, out_ref, o_scr, gate_scr, wout_ref)


def _layer_call(body, x, operands, scratch_shapes, name, with_margin, row_tile=ROW_TILE):
    batch, seq, d_model = x.shape
    tm = min(row_tile, seq)
    assert seq % tm == 0 and tm % SB_BLOCK == 0 and tm % HG_CHUNK == 0
    steps = seq // tm

    def whole(a):
        return pl.BlockSpec(a.shape, lambda b, t: (0,) * a.ndim, pipeline_mode=pl.Buffered(1))

    tile = pl.BlockSpec((1, tm, d_model), lambda b, t: (b, t, 0))
    out_specs, out_shape = tile, jax.ShapeDtypeStruct(x.shape, x.dtype)
    if with_margin:
        out_specs = (tile, pl.BlockSpec((1, MARGIN_ROWS, HEAD_DIM),
                                        lambda b, t: (b * steps + t, 0, 0)))
        out_shape = (out_shape,
                     jax.ShapeDtypeStruct((batch * steps, MARGIN_ROWS, HEAD_DIM), _F32))
    return pl.pallas_call(
        body,
        grid=(batch, steps),
        in_specs=[tile] + [whole(a) for a in operands],
        out_specs=out_specs,
        out_shape=out_shape,
        scratch_shapes=scratch_shapes(tm, seq),
        compiler_params=pltpu.CompilerParams(
            dimension_semantics=("arbitrary", "arbitrary"),
            vmem_limit_bytes=V7X_VMEM_LIMIT_BYTES),
        name=name,
    )(x, *operands)


def _sb_layer(x, norm_gain, w_in, q_gain, k_gain, w_out):
    width = w_out.shape[0]
    n_heads = width // HEAD_DIM
    assert w_in.shape[1] == 4 * width

    def scratch(tm, seq):
        return [
            pltpu.VMEM((tm, x.shape[-1]), _BF16),
            pltpu.VMEM((n_heads, tm, HEAD_DIM), _BF16),
            pltpu.VMEM((n_heads, seq, HEAD_DIM), _BF16),
            pltpu.VMEM((n_heads, seq, HEAD_DIM), _BF16),
            pltpu.VMEM((n_heads, tm, HEAD_DIM), _F32),
            pltpu.VMEM((n_heads, tm, HEAD_DIM), _F32),
            pltpu.VMEM((n_heads, tm, HEAD_DIM), _F32),
        ]

    operands = (norm_gain.reshape(1, -1), w_in.astype(_BF16), q_gain.reshape(1, -1),
                k_gain.reshape(1, -1), w_out.astype(_BF16))

    def run(windowed):
        return _layer_call(
            functools.partial(_sb_layer_kernel, n_heads=n_heads, windowed=windowed),
            x, operands, scratch,
            "stick_breaking_window" if windowed else "stick_breaking_layer", windowed)

    out, margin = run(True)
    return lax.cond(jnp.min(margin) >= SB_DEAD_SKIP_MASS, lambda: out, lambda: run(False))


def _hg_layer(x, norm_gain, w_in, o_gain, w_out, lb_logits, layer):
    width = w_out.shape[0]
    n_heads = width // HEAD_DIM
    assert w_in.shape[1] == 4 * width

    def scratch(guarded, tm, seq):
        per_head = pltpu.VMEM((n_heads, tm, HEAD_DIM), _F32)
        gate_k = per_head if guarded else pltpu.VMEM((1, 8, HEAD_DIM), _F32)
        return ([pltpu.VMEM((tm, x.shape[-1]), _BF16), per_head, per_head, gate_k,
                 per_head, per_head, per_head,
                 pltpu.VMEM((n_heads, HEAD_DIM, HEAD_DIM), _F32)])

    operands = (norm_gain.reshape(1, -1), w_in.astype(_BF16), o_gain.reshape(1, -1),
                w_out.astype(_BF16), lb_logits)

    def run(guarded):
        return _layer_call(
            functools.partial(_hg_layer_kernel, n_heads=n_heads, layer=layer, guarded=guarded),
            x, operands, functools.partial(scratch, guarded),
            "hgrn2_layer" if guarded else "hgrn2_matmul_path", not guarded,
            ROW_TILE if guarded else HG_BRANCH_FREE_ROW_TILE)

    out, margin = run(False)
    return lax.cond(jnp.min(margin) >= -HG_MAX_CHUNK_DECAY, lambda: out, lambda: run(True))


def kernel(x, sb_norm, sb_w_in, sb_q_gain, sb_k_gain, sb_w_out,
           hg_norm, hg_w_in, hg_o_gain, hg_w_out, hg_lb_logits):
    depth = hg_lb_logits.shape[0]
    h = x
    for layer in range(depth):
        j = layer // 2
        if layer % 2 == 0:
            h = _sb_layer(h, sb_norm[j], sb_w_in[j], sb_q_gain[j], sb_k_gain[j], sb_w_out[j])
        else:
            h = _hg_layer(h, hg_norm[j], hg_w_in[j], hg_o_gain[j], hg_w_out[j],
                          hg_lb_logits, layer)
    return h
```

```python
import functools

import jax
import jax.numpy as jnp
from jax import lax
from jax.experimental import pallas as pl
from jax.experimental.pallas import tpu as pltpu

HEAD_DIM = 128
RMS_EPS = 1e-6
LOG2_E = 1.4426950408889634
ROW_TILE = 512
HG_BRANCH_FREE_ROW_TILE = 1024
SB_WINDOW_ROW_TILE = 1024
NORM_ROWS = 32
SB_BLOCK = 256
SB_SOFTPLUS_CAP = 64.0
SB_DEAD_SKIP_MASS = 160.0
HG_CHUNK = 128
HG_SUBCHUNK = 8
HG_MAX_CHUNK_DECAY = 80.0
SB_PIPELINE_SKEW = 2
HG_PIPELINE_SKEW = 4
MARGIN_ROWS = 8
NO_CONSTRAINT = 1e30
V7X_VMEM_LIMIT_BYTES = 56 * 1024 * 1024

_F32 = jnp.float32
_BF16 = jnp.bfloat16
_NT = (((1,), (1,)), ((), ()))
_TN = (((0,), (0,)), ((), ()))


def _rms_rows(x, gain):
    return x * lax.rsqrt(jnp.mean(x * x, axis=-1, keepdims=True) + RMS_EPS) * gain


def _silu(x):
    half = 0.5 * x
    return half + half * jnp.tanh(half)


def _normalise(x_ref, gain, u_scr):
    for r in range(0, u_scr.shape[0], NORM_ROWS):
        u_scr[r:r + NORM_ROWS, :] = _rms_rows(x_ref[0, r:r + NORM_ROWS, :], gain).astype(_BF16)


def _project(u_scr, win_ref, width, group_order, store):
    u = u_scr[...]
    n_heads = width // HEAD_DIM

    def matmul(grp):
        return jnp.dot(u, win_ref[:, grp * width:(grp + 1) * width],
                       preferred_element_type=_F32)

    ahead = matmul(group_order[0])
    for n, grp in enumerate(group_order):
        pg = ahead
        if n + 1 < len(group_order):
            ahead = matmul(group_order[n + 1])
        for h in range(n_heads):
            store(grp, h, pg[:, h * HEAD_DIM:(h + 1) * HEAD_DIM])


def _gated_out(x_ref, out_ref, o_scr, gate_scr, wout_ref):
    o = jnp.concatenate([o_scr[h] * gate_scr[h] for h in range(o_scr.shape[0])], axis=1)
    out_ref[0] = x_ref[0] + jnp.dot(o.astype(_BF16), wout_ref[...], preferred_element_type=_F32)


def _least_rows(x):
    least = x[:MARGIN_ROWS]
    for r in range(MARGIN_ROWS, x.shape[0], MARGIN_ROWS):
        least = jnp.minimum(least, x[r:r + MARGIN_ROWS])
    return least


def _software_pipeline(items, first, second, third, skew):
    a_out, b_out = {}, {}
    for step in range(len(items) + 2 * skew):
        n1, n2, n3 = step, step - skew, step - 2 * skew
        if n1 < len(items):
            a_out[n1] = first(*items[n1])
        if 0 <= n2 < len(items):
            b_out[n2] = second(*items[n2], *a_out.pop(n2))
        if 0 <= n3 < len(items):
            third(*items[n3], *b_out.pop(n3))


def _sb_layer_kernel(x_ref, ng_ref, win_ref, qg_ref, kg_ref, wout_ref, *rest,
                     n_heads, windowed):
    if windowed:
        out_ref, margin_ref = rest[:2]
    else:
        out_ref, margin_ref = rest[0], None
    u_scr, q_scr, k_scr, v_scr, gate_scr, acc_scr, sum_scr = rest[-7:]
    tm = x_ref.shape[1]
    blk = SB_BLOCK
    n_sub = tm // blk
    i = pl.program_id(1)
    row0 = pl.multiple_of(i * tm, tm)
    qg = qg_ref[...] * (HEAD_DIM ** -0.5 * LOG2_E)
    kg = kg_ref[...]

    def store(grp, h, tile):
        if grp == 0:
            q_scr[h] = _rms_rows(tile, qg).astype(_BF16)
        elif grp == 1:
            k_scr[h, pl.ds(row0, tm), :] = _rms_rows(tile, kg).astype(_BF16)
        elif grp == 2:
            v_scr[h, pl.ds(row0, tm), :] = tile.astype(_BF16)
        else:
            gate_scr[h] = _silu(tile)

    _normalise(x_ref, ng_ref[...], u_scr)
    _project(u_scr, win_ref, n_heads * HEAD_DIM, (0, 1, 3, 2), store)

    t_idx = lax.broadcasted_iota(jnp.int32, (blk, blk), 0)
    s_idx = lax.broadcasted_iota(jnp.int32, (blk, blk), 1)
    causal = s_idx < t_idx
    later_keys = jnp.where(causal, 1.0, 0.0).astype(_BF16)
    has_earlier_tiles = i > 0


    def stack(top, rest):
        return jnp.concatenate([top, rest], axis=0) if rest.shape[0] else top

    def gates(h, lo, n_rows, own, r0):
        q = q_scr[h, lo:lo + n_rows, :]
        z2 = lax.dot_general(q, k_scr[h, pl.ds(r0, blk), :], _NT, preferred_element_type=_F32)
        skip = jnp.maximum(
            z2, jnp.log(1.0 + jnp.exp2(jnp.minimum(z2, SB_SOFTPLUS_CAP))) * LOG2_E)
        log_beta = z2 - skip
        if own:
            skip = stack(jnp.where(causal, skip[:own], 0.0), skip[own:])
        row_sum = jnp.sum(skip, axis=-1, keepdims=True)
        return log_beta, skip.astype(_BF16), row_sum

    def weights(h, lo, n_rows, own, r0, log_beta, skip, row_sum):
        skipped = jnp.dot(skip, later_keys, preferred_element_type=_F32)
        parts = []
        if own:
            sum_scr[h, lo:lo + own, :] = jnp.broadcast_to(row_sum[:own], (own, HEAD_DIM))
            parts.append(jnp.where(causal, jnp.exp2(log_beta[:own] - skipped[:own]), 0.0))
        if own < n_rows:
            rest = slice(lo + own, lo + n_rows)
            nearer = sum_scr[h, rest, :]
            sum_scr[h, rest, :] = nearer + row_sum[own:]
            parts.append(jnp.exp2(log_beta[own:] - skipped[own:]
                                  - jnp.concatenate([nearer] * (blk // HEAD_DIM), axis=1)))
        return (jnp.concatenate(parts, axis=0).astype(_BF16),)

    def accumulate(h, lo, n_rows, own, r0, w):
        o = jnp.dot(w, v_scr[h, pl.ds(r0, blk), :], preferred_element_type=_F32)
        if own:
            acc_scr[h, lo:lo + own, :] = o[:own]
        if own < n_rows:
            acc_scr[h, lo + own:lo + n_rows, :] += o[own:]

    def least_skip_mass(a):
        rows = slice(a * blk, (a + 1) * blk)
        least = sum_scr[0, rows, :]
        for h in range(1, n_heads):
            least = jnp.minimum(least, sum_scr[h, rows, :])
        return least

    in_tile = [(h, kb * blk, tm - kb * blk, blk, row0 + kb * blk)
               for kb in range(n_sub - 1, -1, -1) for h in range(n_heads)]


    if windowed:
        near = [(h, kb * blk, min(2 * blk, tm - kb * blk), blk, row0 + kb * blk)
                for kb in range(n_sub - 1, -1, -1) for h in range(n_heads)]

        def window_with_before():
            before = [(h, 0, blk, 0, pl.multiple_of(row0 - blk, blk)) for h in range(n_heads)]
            _software_pipeline(near + before, gates, weights, accumulate, SB_PIPELINE_SKEW)

        def window_of_first_tile():
            _software_pipeline(near, gates, weights, accumulate, SB_PIPELINE_SKEW)

        lax.cond(has_earlier_tiles, window_with_before, window_of_first_tile)
        always = [least_skip_mass(a) for a in range(2, n_sub)]
        later_tiles = [least_skip_mass(a) for a in range(min(2, n_sub))]
        least = functools.reduce(jnp.minimum, later_tiles)
        least = jnp.where(has_earlier_tiles, _least_rows(least), NO_CONSTRAINT)
        if always:
            least = jnp.minimum(least, _least_rows(functools.reduce(jnp.minimum, always)))
        margin_ref[0] = least
    else:
        _software_pipeline(in_tile, gates, weights, accumulate, SB_PIPELINE_SKEW)
        for a in range(n_sub):
            def kv_cond(state):
                n, least = state
                return jnp.logical_and(n < n_sub * i, least < SB_DEAD_SKIP_MASS)

            def kv_body(state, a=a):
                n, _ = state
                r0 = pl.multiple_of(row0 - (n + 1) * blk, blk)
                earlier = [(h, a * blk, blk, 0, r0) for h in range(n_heads)]
                _software_pipeline(earlier, gates, weights, accumulate, SB_PIPELINE_SKEW)
                return n + 1, jnp.min(least_skip_mass(a))

            least = jnp.min(least_skip_mass(a)) if a > 0 else jnp.float32(0.0)
            lax.while_loop(kv_cond, kv_body, (jnp.int32(0), least))

    _gated_out(x_ref, out_ref, acc_scr, gate_scr, wout_ref)


def _hg_layer_kernel(x_ref, ng_ref, win_ref, og_ref, wout_ref, lbl_ref, *rest,
                     n_heads, layer, guarded):
    if guarded:
        out_ref, margin_ref = rest[0], None
    else:
        out_ref, margin_ref = rest[:2]
    u_scr, q_scr, g_scr, k_scr, i_scr, gate_scr, o_scr, st_scr = rest[-8:]
    tm = x_ref.shape[1]
    n_chunks = tm // HG_CHUNK
    og = og_ref[...]

    @pl.when(pl.program_id(1) == 0)
    def _():
        st_scr[...] = jnp.zeros_like(st_scr)

    lg = lbl_ref[...]
    lmax = jnp.max(lg, axis=0, keepdims=True)
    le = jnp.exp(lg - lmax)
    lb_row = jnp.sum(le[1:layer + 1], axis=0, keepdims=True) / jnp.sum(le, axis=0, keepdims=True)

    chunk_min = [None]

    def note_chunk_total(tot):
        chunk_min[0] = tot if chunk_min[0] is None else jnp.minimum(chunk_min[0], tot)

    def forget_gates(f_pre, h):
        lb = lb_row[:, h * HEAD_DIM:(h + 1) * HEAD_DIM]
        half_span = 0.5 * (1.0 - lb)
        f = (lb + half_span) + half_span * jnp.tanh(0.5 * f_pre)
        f = jnp.maximum(f, lb)
        return jnp.log(f), 1.0 - f

    def store(grp, h, tile):
        if grp == 0:
            q_scr[h] = _silu(tile)
        elif grp == 1 and guarded:
            g, k = forget_gates(tile, h)
            g_scr[h] = g
            k_scr[h] = k
            for c in range(n_chunks):
                note_chunk_total(jnp.sum(g[c * HG_CHUNK:(c + 1) * HG_CHUNK], axis=0, keepdims=True))
        elif grp == 1:
            g_scr[h] = tile
        elif grp == 2:
            i_scr[h] = tile
        else:
            gate_scr[h] = _silu(tile)

    _normalise(x_ref, ng_ref[...], u_scr)
    _project(u_scr, win_ref, n_heads * HEAD_DIM, (1, 0, 3, 2), store)

    def matmul_path():
        c_t = lax.broadcasted_iota(jnp.int32, (HG_CHUNK, HG_CHUNK), 0)
        c_s = lax.broadcasted_iota(jnp.int32, (HG_CHUNK, HG_CHUNK), 1)
        incl = c_s <= c_t
        earlier_rows = jnp.where(incl, 1.0, 0.0).astype(_BF16)


        def decays(c, h):
            rows = pl.ds(c * HG_CHUNK, HG_CHUNK)
            q = q_scr[h, rows, :]
            if guarded:
                g, k = g_scr[h, rows, :], k_scr[h, rows, :]
            else:
                g, k = forget_gates(g_scr[h, rows, :], h)
            gc = jnp.dot(earlier_rows, g.astype(_BF16), preferred_element_type=_F32)
            g_last = gc[HG_CHUNK - 1:HG_CHUNK, :]
            g_mid = gc[HG_CHUNK // 2 - 1:HG_CHUNK // 2, :]
            if not guarded:
                note_chunk_total(g_last)
            to_mid = jnp.exp(gc - g_mid)
            q_mid = q * to_mid
            k_mid = k / to_mid
            q_in = q_mid * jnp.exp(g_mid)
            k_end = k_mid * jnp.exp(g_last - g_mid)
            return (q_mid.astype(_BF16), k_mid.astype(_BF16), q_in.astype(_BF16),
                    k_end.astype(_BF16), jnp.exp(g_last))

        def intra_scores(c, h, q_mid, k_mid, q_in, k_end, decay_last):
            scores = lax.dot_general(q_mid, k_mid, _NT, preferred_element_type=_F32)
            return jnp.where(incl, scores, 0.0).astype(_BF16), q_in, k_end, decay_last

        def outputs(c, h, scores, q_in, k_end, decay_last):
            rows = pl.ds(c * HG_CHUNK, HG_CHUNK)
            iv = i_scr[h, rows, :].astype(_BF16)
            st = st_scr[h]
            o = jnp.dot(scores, iv, preferred_element_type=_F32)
            o = o + jnp.dot(q_in, st.T.astype(_BF16), preferred_element_type=_F32)
            st_scr[h] = decay_last * st + lax.dot_general(
                iv, k_end, _TN, preferred_element_type=_F32)
            o_scr[h, rows, :] = _rms_rows(o, og)

        items = [(c, h) for c in range(n_chunks) for h in range(n_heads)]
        _software_pipeline(items, decays, intra_scores, outputs, HG_PIPELINE_SKEW)

    def pairwise_path():
        sc = HG_SUBCHUNK
        r_t = lax.broadcasted_iota(jnp.int32, (sc, HEAD_DIM), 0)

        def head_body(h, carry_unused):
            def sub_body(c, st):
                rows = pl.ds(pl.multiple_of(c * sc, sc), sc)
                q = q_scr[h, rows, :]
                k = k_scr[h, rows, :]
                iv = i_scr[h, rows, :]
                g = g_scr[h, rows, :]
                gc = g
                for shift in (1, 2, 4):
                    moved = pltpu.roll(gc, shift, 0)
                    gc = gc + jnp.where(r_t >= shift, moved, 0.0)
                g_last = gc[sc - 1:sc, :]
                o = lax.dot_general((q * jnp.exp(gc)).astype(_BF16), st.astype(_BF16),
                                    _NT, preferred_element_type=_F32)
                for s in range(sc):
                    decay = jnp.exp(jnp.minimum(gc - gc[s:s + 1, :], 0.0))
                    pair = jnp.sum(q * k[s:s + 1, :] * decay, axis=-1, keepdims=True)
                    o = o + jnp.where(r_t >= s, pair, 0.0) * iv[s:s + 1, :]
                k_end = (k * jnp.exp(g_last - gc)).astype(_BF16)
                st = jnp.exp(g_last) * st + lax.dot_general(
                    iv.astype(_BF16), k_end, _TN, preferred_element_type=_F32)
                o_scr[h, rows, :] = _rms_rows(o, og)
                return st

            st_scr[h] = lax.fori_loop(0, tm // sc, sub_body, st_scr[h])
            return carry_unused

        lax.fori_loop(0, n_heads, head_body, 0)

    if guarded:
        lax.cond(jnp.min(chunk_min[0]) >= -HG_MAX_CHUNK_DECAY, matmul_path, pairwise_path)
    else:
        matmul_path()
        margin_ref[0] = jnp.broadcast_to(chunk_min[0], (MARGIN_ROWS, HEAD_DIM))
    _gated_out(x_ref, out_ref, o_scr, gate_scr, wout_ref)


def _layer_call(body, x, operands, scratch_shapes, name, with_margin, row_tile=ROW_TILE):
    batch, seq, d_model = x.shape
    tm = min(row_tile, seq)
    assert seq % tm == 0 and tm % SB_BLOCK == 0 and tm % HG_CHUNK == 0
    steps = seq // tm

    def whole(a):
        return pl.BlockSpec(a.shape, lambda b, t: (0,) * a.ndim, pipeline_mode=pl.Buffered(1))

    tile = pl.BlockSpec((1, tm, d_model), lambda b, t: (b, t, 0))
    out_specs, out_shape = tile, jax.ShapeDtypeStruct(x.shape, x.dtype)
    if with_margin:
        out_specs = (tile, pl.BlockSpec((1, MARGIN_ROWS, HEAD_DIM),
                                        lambda b, t: (b * steps + t, 0, 0)))
        out_shape = (out_shape,
                     jax.ShapeDtypeStruct((batch * steps, MARGIN_ROWS, HEAD_DIM), _F32))
    return pl.pallas_call(
        body,
        grid=(batch, steps),
        in_specs=[tile] + [whole(a) for a in operands],
        out_specs=out_specs,
        out_shape=out_shape,
        scratch_shapes=scratch_shapes(tm, seq),
        compiler_params=pltpu.CompilerParams(
            dimension_semantics=("arbitrary", "arbitrary"),
            vmem_limit_bytes=V7X_VMEM_LIMIT_BYTES),
        name=name,
    )(x, *operands)


def _sb_layer(x, norm_gain, w_in, q_gain, k_gain, w_out):
    width = w_out.shape[0]
    n_heads = width // HEAD_DIM
    assert w_in.shape[1] == 4 * width

    def scratch(tm, seq):
        return [
            pltpu.VMEM((tm, x.shape[-1]), _BF16),
            pltpu.VMEM((n_heads, tm, HEAD_DIM), _BF16),
            pltpu.VMEM((n_heads, seq, HEAD_DIM), _BF16),
            pltpu.VMEM((n_heads, seq, HEAD_DIM), _BF16),
            pltpu.VMEM((n_heads, tm, HEAD_DIM), _F32),
            pltpu.VMEM((n_heads, tm, HEAD_DIM), _F32),
            pltpu.VMEM((n_heads, tm, HEAD_DIM), _F32),
        ]

    operands = (norm_gain.reshape(1, -1), w_in.astype(_BF16), q_gain.reshape(1, -1),
                k_gain.reshape(1, -1), w_out.astype(_BF16))

    def run(windowed):
        return _layer_call(
            functools.partial(_sb_layer_kernel, n_heads=n_heads, windowed=windowed),
            x, operands, scratch,
            "stick_breaking_window" if windowed else "stick_breaking_layer", windowed,
            SB_WINDOW_ROW_TILE if windowed else ROW_TILE)

    out, margin = run(True)
    return lax.cond(jnp.min(margin) >= SB_DEAD_SKIP_MASS, lambda: out, lambda: run(False))


def _hg_layer(x, norm_gain, w_in, o_gain, w_out, lb_logits, layer):
    width = w_out.shape[0]
    n_heads = width // HEAD_DIM
    assert w_in.shape[1] == 4 * width

    def scratch(guarded, tm, seq):
        per_head = pltpu.VMEM((n_heads, tm, HEAD_DIM), _F32)
        gate_k = per_head if guarded else pltpu.VMEM((1, 8, HEAD_DIM), _F32)
        return ([pltpu.VMEM((tm, x.shape[-1]), _BF16), per_head, per_head, gate_k,
                 per_head, per_head, per_head,
                 pltpu.VMEM((n_heads, HEAD_DIM, HEAD_DIM), _F32)])

    operands = (norm_gain.reshape(1, -1), w_in.astype(_BF16), o_gain.reshape(1, -1),
                w_out.astype(_BF16), lb_logits)

    def run(guarded):
        return _layer_call(
            functools.partial(_hg_layer_kernel, n_heads=n_heads, layer=layer, guarded=guarded),
            x, operands, functools.partial(scratch, guarded),
            "hgrn2_layer" if guarded else "hgrn2_matmul_path", not guarded,
            ROW_TILE if guarded else HG_BRANCH_FREE_ROW_TILE)

    out, margin = run(False)
    return lax.cond(jnp.min(margin) >= -HG_MAX_CHUNK_DECAY, lambda: out, lambda: run(True))


def kernel(x, sb_norm, sb_w_in, sb_q_gain, sb_k_gain, sb_w_out,
           hg_norm, hg_w_in, hg_o_gain, hg_w_out, hg_lb_logits):
    depth = hg_lb_logits.shape[0]
    h = x
    for layer in range(depth):
        j = layer // 2
        if layer % 2 == 0:
            h = _sb_layer(h, sb_norm[j], sb_w_in[j], sb_q_gain[j], sb_k_gain[j], sb_w_out[j])
        else:
            h = _hg_layer(h, hg_norm[j], hg_w_in[j], hg_o_gain[j], hg_w_out[j],
                          hg_lb_logits, layer)
    return h
```
